```python
import math
import jax, jax.numpy as jnp
from jax import lax
import numpy as np


D_MODEL = 1024
BATCH = 8
SEQ = 8192
DEPTH = 1
DEC_BATCH = 4
DEC_SEQ = 8192
PAST_LEN = 128

N_MEM = 256
DIFF_HEADS = 8
DIFF_DK = 64
DIFF_DV = 2 * DIFF_DK
DIL_CONFIGS = ((128, 1), (512, 4), (2048, 16))
DIL_HEADS_PER_GROUP = 4
DIL_HD = 64
MEM_HEADS = 4
MEM_HD = 64
N_BRANCHES = 3
N_EXPERTS = 256
TOP_K = 8
D_EXPERT = 256
ROUTED_SCALE = 2.5
EXPERT_BLOCK = 128
Q_BLOCK = 128
LN_EPS = 1e-5
NEG_INF = -1e30
DEEPNORM_ALPHA = (2 * DEPTH) ** 0.25
DEEPNORM_BETA = (8 * DEPTH) ** -0.25

DIFF_QK = DIFF_HEADS * 2 * DIFF_DK
DIFF_V = DIFF_HEADS * DIFF_DV
N_DIL_HEADS = len(DIL_CONFIGS) * DIL_HEADS_PER_GROUP
DIL_W = N_DIL_HEADS * DIL_HD
DIL_OUT = DIL_HEADS_PER_GROUP * DIL_HD
MEM_Q = MEM_HEADS * MEM_HD
D_IN = 2 * DIFF_QK + DIFF_V + 3 * DIL_W + MEM_Q
SPLIT_POINTS = (DIFF_QK, 2 * DIFF_QK, 2 * DIFF_QK + DIFF_V, 2 * DIFF_QK + DIFF_V + DIL_W,
                2 * DIFF_QK + DIFF_V + 2 * DIL_W, 2 * DIFF_QK + DIFF_V + 3 * DIL_W)

kernel_name = 'hybrid_diff_dilated_moe_encoder'


def _layernorm(x, g, b):
    xf = x.astype(jnp.float32)
    mu = jnp.mean(xf, axis=-1, keepdims=True)
    var = jnp.mean(jnp.square(xf - mu), axis=-1, keepdims=True)
    y = (xf - mu) * lax.rsqrt(var + LN_EPS) * g.astype(jnp.float32) + b.astype(jnp.float32)
    return y.astype(x.dtype)


def _rmsnorm(xf, g):
    return xf * lax.rsqrt(jnp.mean(jnp.square(xf), axis=-1, keepdims=True) + LN_EPS) * g.astype(jnp.float32)


def _alibi_slopes(n):
    return jnp.asarray(2.0 ** (-8.0 * np.arange(1, n + 1) / n), dtype=jnp.float32)


def _diff_attention(q, k, v, slopes, lam):
    B, S, H, _, dk = q.shape
    dv = v.shape[-1]
    nq = S // Q_BLOCK
    qb = q.reshape(B, nq, Q_BLOCK, H, 2, dk).transpose(1, 0, 2, 3, 4, 5)
    v32 = v.astype(jnp.float32)
    kpos = jnp.arange(S, dtype=jnp.int32)
    scale = dk ** -0.5

    def block(args):
        qi, i = args
        s = jnp.einsum('bqhcd,bkhcd->bhcqk', qi, k).astype(jnp.float32) * scale
        qpos = i * Q_BLOCK + jnp.arange(Q_BLOCK, dtype=jnp.int32)
        dist = jnp.abs(qpos[:, None] - kpos[None, :]).astype(jnp.float32)
        s = s - slopes[None, :, None, None, None] * dist
        p = jax.nn.softmax(s, axis=-1)
        a = p[:, :, 0] - lam * p[:, :, 1]
        return jnp.einsum('bhqk,bkhd->bqhd', a, v32)

    o = lax.map(block, (qb, jnp.arange(nq, dtype=jnp.int32)))
    return o.transpose(1, 0, 2, 3, 4).reshape(B, S, H, dv)


def _banded_attention(q, k, v, slopes, w):
    N, L, H, hd = q.shape
    nb = -(-L // w)
    Lp = nb * w
    pad = Lp - L
    qp = jnp.pad(q, ((0, 0), (0, pad), (0, 0), (0, 0)))
    kp = jnp.pad(k, ((0, 0), (w, pad + w), (0, 0), (0, 0)))
    vp = jnp.pad(v, ((0, 0), (w, pad + w), (0, 0), (0, 0)))
    qb = qp.reshape(N, nb, w, H, hd)

    def windows(t):
        return jnp.concatenate([t[:, j * w: j * w + Lp].reshape(N, nb, w, H, hd) for j in range(3)], axis=2)

    kb, vb = windows(kp), windows(vp)
    s = jnp.einsum('nbqhd,nbkhd->nbhqk', qb, kb).astype(jnp.float32) * (hd ** -0.5)
    qpos = jnp.arange(nb, dtype=jnp.int32)[:, None] * w + jnp.arange(w, dtype=jnp.int32)[None, :]
    kpos = jnp.arange(nb, dtype=jnp.int32)[:, None] * w - w + jnp.arange(3 * w, dtype=jnp.int32)[None, :]
    rel = jnp.abs(qpos[:, :, None] - kpos[:, None, :])
    ok = (rel <= w) & (kpos[:, None, :] >= 0) & (kpos[:, None, :] < L)
    s = s - slopes[None, None, :, None, None] * rel.astype(jnp.float32)[None, :, None]
    s = jnp.where(ok[None, :, None], s, NEG_INF)
    lse = jax.nn.logsumexp(s, axis=-1)
    p = jnp.exp(s - lse[..., None])
    o = jnp.einsum('nbhqk,nbkhd->nbqhd', p, vb.astype(jnp.float32))
    o = o.reshape(N, Lp, H, hd)[:, :L]
    lse = lse.transpose(0, 1, 3, 2).reshape(N, Lp, H)[:, :L]
    return o, lse


def _dilated_group(q, k, v, slopes, window, dilation):
    B, S, H, hd = q.shape
    L = S // dilation
    half = window // (2 * dilation)

    def to_sub(t):
        return t.reshape(B, L, dilation, H, hd).transpose(0, 2, 1, 3, 4).reshape(B * dilation, L, H, hd)

    o, lse = _banded_attention(to_sub(q), to_sub(k), to_sub(v), slopes * dilation, half)
    o = o.reshape(B, dilation, L, H, hd).transpose(0, 2, 1, 3, 4).reshape(B, S, H, hd)
    lse = lse.reshape(B, dilation, L, H).transpose(0, 2, 1, 3).reshape(B, S, H)
    return o, lse


def _memory_attention(q, mem_n, w_mem_kv):
    B, M, _ = mem_n.shape
    kv = (mem_n @ w_mem_kv).reshape(B, M, 2, MEM_HEADS, MEM_HD)
    s = jnp.einsum('bqhd,bkhd->bhqk', q, kv[:, :, 0]).astype(jnp.float32) * (MEM_HD ** -0.5)
    p = jax.nn.softmax(s, axis=-1)
    return jnp.einsum('bhqk,bkhd->bqhd', p, kv[:, :, 1].astype(jnp.float32))


def _moe(x, w_router, router_bias, w_e_gate, w_e_up, w_e_down, w_s_gate, w_s_up, w_s_down):
    B, S, D = x.shape
    T = B * S
    xt = x.reshape(T, D)
    scores = jax.nn.sigmoid((xt @ w_router).astype(jnp.float32))
    _, top_idx = lax.top_k(scores + router_bias.astype(jnp.float32), TOP_K)
    top_w = jnp.take_along_axis(scores, top_idx, axis=-1)
    top_w = top_w / jnp.sum(top_w, axis=-1, keepdims=True) * ROUTED_SCALE
    n_assign = T * TOP_K
    e_flat = top_idx.reshape(n_assign)
    t_flat = jnp.repeat(jnp.arange(T, dtype=jnp.int32), TOP_K)
    w_flat = top_w.reshape(n_assign)
    order = jnp.argsort(e_flat)
    e_s, t_s, w_s = e_flat[order], t_flat[order], w_flat[order]
    counts = jnp.bincount(e_flat, length=N_EXPERTS)
    padded = (counts + EXPERT_BLOCK - 1) // EXPERT_BLOCK * EXPERT_BLOCK
    start = jnp.cumsum(counts) - counts
    pend = jnp.cumsum(padded)
    pstart = pend - padded
    dest = pstart[e_s] + jnp.arange(n_assign, dtype=jnp.int32) - start[e_s]
    n_blocks = -(-n_assign // EXPERT_BLOCK) + N_EXPERTS
    n_rows = n_blocks * EXPERT_BLOCK
    tok_buf = jnp.full((n_rows,), T, dtype=jnp.int32).at[dest].set(t_s)
    w_buf = jnp.zeros((n_rows,), jnp.float32).at[dest].set(w_s)
    blk_expert = jnp.minimum(jnp.searchsorted(pend, jnp.arange(n_blocks, dtype=jnp.int32) * EXPERT_BLOCK, side='right'), N_EXPERTS - 1)
    x_pad = jnp.concatenate([xt, jnp.zeros((1, D), xt.dtype)], axis=0)

    def expert_block(args):
        tok, e = args
        rows = x_pad[tok]
        h = jax.nn.silu(rows @ w_e_gate[e]) * (rows @ w_e_up[e])
        return h @ w_e_down[e]

    ys = lax.map(expert_block, (tok_buf.reshape(n_blocks, EXPERT_BLOCK), blk_expert))
    ys = ys.reshape(n_rows, D).astype(jnp.float32) * w_buf[:, None]
    routed = jnp.zeros((T + 1, D), jnp.float32).at[tok_buf].add(ys)[:T]
    shared = (jax.nn.silu(xt @ w_s_gate) * (xt @ w_s_up)) @ w_s_down
    return (routed.astype(x.dtype) + shared).reshape(B, S, D)


def _layer(x, mem, l, ln_mem_g, ln_mem_b, w_in, w_gate, b_gate, lambda_q1, lambda_k1, lambda_q2, lambda_k2,
           diff_subln_g, w_br_diff, w_br_dil, w_br_mem, w_mem_kv, w_out, ln1_g, ln1_b,
           w_router, router_bias, w_e_gate, w_e_up, w_e_down, w_s_gate, w_s_up, w_s_down, ln2_g, ln2_b):
    B, S, D = x.shape
    proj = x @ w_in
    q_d, k_d, v_d, q_l, k_l, v_l, q_m = jnp.split(proj, list(SPLIT_POINTS), axis=-1)

    lambda_init = 0.8 - 0.6 * math.exp(-0.3 * l)
    lam = (jnp.exp(jnp.sum(lambda_q1.astype(jnp.float32) * lambda_k1.astype(jnp.float32)))
           - jnp.exp(jnp.sum(lambda_q2.astype(jnp.float32) * lambda_k2.astype(jnp.float32))) + lambda_init)
    o_d = _diff_attention(q_d.reshape(B, S, DIFF_HEADS, 2, DIFF_DK), k_d.reshape(B, S, DIFF_HEADS, 2, DIFF_DK),
                          v_d.reshape(B, S, DIFF_HEADS, DIFF_DV), _alibi_slopes(DIFF_HEADS), lam)
    o_d = _rmsnorm(o_d, diff_subln_g) * (1.0 - lambda_init)
    br_diff = o_d.reshape(B, S, DIFF_V).astype(x.dtype) @ w_br_diff

    q_l = q_l.reshape(B, S, N_DIL_HEADS, DIL_HD)
    k_l = k_l.reshape(B, S, N_DIL_HEADS, DIL_HD)
    v_l = v_l.reshape(B, S, N_DIL_HEADS, DIL_HD)
    slopes = _alibi_slopes(N_DIL_HEADS)
    outs, lses = [], []
    for g, (window, dilation) in enumerate(DIL_CONFIGS):
        hs = slice(g * DIL_HEADS_PER_GROUP, (g + 1) * DIL_HEADS_PER_GROUP)
        o, lse = _dilated_group(q_l[:, :, hs], k_l[:, :, hs], v_l[:, :, hs], slopes[hs], window, dilation)
        outs.append(o)
        lses.append(lse)
    wts = jax.nn.softmax(jnp.stack(lses), axis=0)
    o_l = jnp.sum(wts[..., None] * jnp.stack(outs), axis=0)
    br_dil = o_l.reshape(B, S, DIL_OUT).astype(x.dtype) @ w_br_dil

    mem_n = _layernorm(mem, ln_mem_g, ln_mem_b)
    o_m = _memory_attention(q_m.reshape(B, S, MEM_HEADS, MEM_HD), mem_n, w_mem_kv)
    br_mem = o_m.reshape(B, S, MEM_Q).astype(x.dtype) @ w_br_mem

    gates = jax.nn.sigmoid((x @ w_gate + b_gate).astype(jnp.float32)).astype(x.dtype).reshape(B, S, N_BRANCHES, D)
    merged = gates[:, :, 0] * br_diff + gates[:, :, 1] * br_dil + gates[:, :, 2] * br_mem
    x = _layernorm(DEEPNORM_ALPHA * x + merged @ w_out, ln1_g, ln1_b)

    y = _moe(x, w_router, router_bias, w_e_gate, w_e_up, w_e_down, w_s_gate, w_s_up, w_s_down)
    return _layernorm(DEEPNORM_ALPHA * x + y, ln2_g, ln2_b)


def _trunk(x, mem, ln_in_g, ln_in_b, layer_params):
    x = _layernorm(x, ln_in_g, ln_in_b)
    for l in range(DEPTH):
        x = _layer(x, mem, l, *[p[l] for p in layer_params])
    return x


def _normal(key, shape, scale):
    return scale * jax.random.normal(key, shape, jnp.float32)


def setup_inputs(seed: int = 0) -> dict:
    key = jax.random.key(seed)
    ks = jax.random.split(key, 40)
    D, L, E, F = D_MODEL, DEPTH, N_EXPERTS, D_EXPERT
    return {
        'x_prompt': _normal(ks[0], (BATCH, SEQ, D), 1.0),
        'x_sample': _normal(ks[1], (DEC_BATCH, DEC_SEQ, D), 1.0),
        'mem_prompt': _normal(ks[2], (BATCH, N_MEM, D), 1.0),
        'mem_sample': _normal(ks[3], (DEC_BATCH, N_MEM, D), 1.0),
        'ln_in_g': 1.0 + _normal(ks[4], (D,), 0.02),
        'ln_in_b': _normal(ks[5], (D,), 0.02),
        'ln_mem_g': 1.0 + _normal(ks[6], (L, D), 0.02),
        'ln_mem_b': _normal(ks[7], (L, D), 0.02),
        'w_in': _normal(ks[8], (L, D, D_IN), D ** -0.5),
        'w_gate': _normal(ks[9], (L, D, N_BRANCHES * D), D ** -0.5),
        'b_gate': _normal(ks[10], (L, N_BRANCHES * D), 0.02),
        'lambda_q1': _normal(ks[11], (L, DIFF_DK), 0.1),
        'lambda_k1': _normal(ks[12], (L, DIFF_DK), 0.1),
        'lambda_q2': _normal(ks[13], (L, DIFF_DK), 0.1),
        'lambda_k2': _normal(ks[14], (L, DIFF_DK), 0.1),
        'diff_subln_g': 1.0 + _normal(ks[15], (L, DIFF_DV), 0.02),
        'w_br_diff': _normal(ks[16], (L, DIFF_V, D), DIFF_V ** -0.5),
        'w_br_dil': _normal(ks[17], (L, DIL_OUT, D), DIL_OUT ** -0.5),
        'w_br_mem': _normal(ks[18], (L, MEM_Q, D), MEM_Q ** -0.5),
        'w_mem_kv': _normal(ks[19], (L, D, 2 * MEM_Q), D ** -0.5),
        'w_out': _normal(ks[20], (L, D, D), D ** -0.5 * DEEPNORM_BETA),
        'ln1_g': 1.0 + _normal(ks[21], (L, D), 0.02),
        'ln1_b': _normal(ks[22], (L, D), 0.02),
        'w_router': _normal(ks[23], (L, D, E), D ** -0.5),
        'router_bias': _normal(ks[24], (L, E), 0.01),
        'w_e_gate': _normal(ks[25], (L, E, D, F), D ** -0.5),
        'w_e_up': _normal(ks[26], (L, E, D, F), D ** -0.5),
        'w_e_down': _normal(ks[27], (L, E, F, D), F ** -0.5 * DEEPNORM_BETA),
        'w_s_gate': _normal(ks[28], (L, D, F), D ** -0.5),
        'w_s_up': _normal(ks[29], (L, D, F), D ** -0.5),
        'w_s_down': _normal(ks[30], (L, F, D), F ** -0.5 * DEEPNORM_BETA),
        'ln2_g': 1.0 + _normal(ks[31], (L, D), 0.02),
        'ln2_b': _normal(ks[32], (L, D), 0.02),
    }


def reference(x_prompt, x_sample, mem_prompt, mem_sample, ln_in_g, ln_in_b, ln_mem_g, ln_mem_b, w_in, w_gate, b_gate,
              lambda_q1, lambda_k1, lambda_q2, lambda_k2, diff_subln_g, w_br_diff, w_br_dil, w_br_mem, w_mem_kv,
              w_out, ln1_g, ln1_b, w_router, router_bias, w_e_gate, w_e_up, w_e_down, w_s_gate, w_s_up, w_s_down,
              ln2_g, ln2_b):
    layer_params = (ln_mem_g, ln_mem_b, w_in, w_gate, b_gate, lambda_q1, lambda_k1, lambda_q2, lambda_k2,
                    diff_subln_g, w_br_diff, w_br_dil, w_br_mem, w_mem_kv, w_out, ln1_g, ln1_b,
                    w_router, router_bias, w_e_gate, w_e_up, w_e_down, w_s_gate, w_s_up, w_s_down, ln2_g, ln2_b)
    y_prompt = _trunk(x_prompt, mem_prompt, ln_in_g, ln_in_b, layer_params)
    y_sample = _trunk(x_sample, mem_sample, ln_in_g, ln_in_b, layer_params)
    return (y_prompt, y_sample)
```

```python
import functools
import math

import numpy as np
import jax
import jax.numpy as jnp
from jax import lax
from jax.experimental import pallas as pl
from jax.experimental.pallas import tpu as pltpu

F32 = jnp.float32
BF16 = jnp.bfloat16

DIFF_HEADS = 8
DIFF_DK = 64
DIFF_DV = 128
DIL_CONFIGS = ((128, 1), (512, 4), (2048, 16))
DIL_HEADS_PER_GROUP = 4
DIL_HD = 64
MEM_HEADS = 4
MEM_HD = 64
TOP_K = 8
ROUTED_SCALE = 2.5
LN_EPS = 1e-5
NEG_INF = -1e30
DEPTH = 1
DEEPNORM_ALPHA = (2 * DEPTH) ** 0.25

D_MODEL = 1024
DIFF_QK = DIFF_HEADS * 2 * DIFF_DK
DIFF_V = DIFF_HEADS * DIFF_DV
N_DIL_HEADS = len(DIL_CONFIGS) * DIL_HEADS_PER_GROUP
DIL_W = N_DIL_HEADS * DIL_HD
DIL_OUT = DIL_HEADS_PER_GROUP * DIL_HD
MEM_Q = MEM_HEADS * MEM_HD
D_IN = 2 * DIFF_QK + DIFF_V + 3 * DIL_W + MEM_Q
OFF_QD, OFF_KD, OFF_VD = 0, DIFF_QK, 2 * DIFF_QK
OFF_QL = 2 * DIFF_QK + DIFF_V
OFF_KL = OFF_QL + DIL_W
OFF_VL = OFF_KL + DIL_W
OFF_QM = OFF_VL + DIL_W

VMEM_LIMIT = 56 * 1024 * 1024
EXPERT_ROWS = 256


def _cparams(sem):
    return pltpu.CompilerParams(dimension_semantics=sem, vmem_limit_bytes=VMEM_LIMIT)


def _ln(x, g, b):
    mu = jnp.mean(x, axis=-1, keepdims=True)
    xc = x - mu
    var = jnp.mean(xc * xc, axis=-1, keepdims=True)
    return xc * lax.rsqrt(var + LN_EPS) * g + b


def _alibi_slopes(n):
    return np.asarray(2.0 ** (-8.0 * np.arange(1, n + 1) / n), dtype=np.float32)


def _ln_proj_kernel(x_ref, g_ref, b_ref, w_ref, o_ref, xn_ref, *, tn):
    xn_ref[...] = _ln(x_ref[...], g_ref[...], b_ref[...]).astype(BF16)
    for n in range(o_ref.shape[1] // tn):
        cols = slice(n * tn, (n + 1) * tn)
        o_ref[:, cols] = jnp.dot(xn_ref[...], w_ref[:, cols], preferred_element_type=F32).astype(o_ref.dtype)


def _ln_proj(x, g, b, w, tm=512, tn=512):
    T, D = x.shape
    N = w.shape[1]
    return pl.pallas_call(
        functools.partial(_ln_proj_kernel, tn=tn),
        grid=(T // tm,),
        in_specs=[pl.BlockSpec((tm, D), lambda i: (i, 0)),
                  pl.BlockSpec((1, D), lambda i: (0, 0)),
                  pl.BlockSpec((1, D), lambda i: (0, 0)),
                  pl.BlockSpec((D, N), lambda i: (0, 0))],
        out_specs=pl.BlockSpec((tm, N), lambda i: (i, 0)),
        out_shape=jax.ShapeDtypeStruct((T, N), BF16),
        scratch_shapes=[pltpu.VMEM((tm, D), BF16)],
        compiler_params=_cparams(("parallel",)),
    )(x, g, b, w)


def _diff_attn_kernel(slope_ref, lam_ref, q_ref, k_ref, v_ref, g_ref, o_ref, m_ref, l_ref, acc_ref,
                      *, tq, tk, nk, lambda_init):
    h = pl.program_id(1)
    qi = pl.program_id(2)
    kj = pl.program_id(3)

    @pl.when(kj == 0)
    def _():
        m_ref[...] = jnp.full(m_ref.shape, NEG_INF, F32)
        l_ref[...] = jnp.zeros(l_ref.shape, F32)
        acc_ref[...] = jnp.zeros(acc_ref.shape, F32)

    q = q_ref[0]
    k = k_ref[0]
    v = v_ref[0]
    lane = lax.broadcasted_iota(jnp.int32, q.shape, 1)
    qpos = qi * tq + lax.broadcasted_iota(jnp.int32, (tq, tk), 0)
    kpos = kj * tk + lax.broadcasted_iota(jnp.int32, (tq, tk), 1)
    bias = slope_ref[h] * jnp.abs(qpos - kpos).astype(F32)
    for c in range(2):
        qz = jnp.where((lane >= c * DIFF_DK) & (lane < (c + 1) * DIFF_DK), q, jnp.zeros_like(q))
        s = lax.dot_general(qz, k, (((1,), (1,)), ((), ())), preferred_element_type=F32) - bias
        m_old = m_ref[c]
        m_new = jnp.maximum(m_old, jnp.max(s, axis=-1, keepdims=True))
        alpha = jnp.exp(m_old - m_new)
        p = jnp.exp(s - m_new)
        l_ref[c] = alpha * l_ref[c] + jnp.sum(p, axis=-1, keepdims=True)
        acc_ref[c] = alpha * acc_ref[c] + jnp.dot(p.astype(BF16), v, preferred_element_type=F32)
        m_ref[c] = m_new

    @pl.when(kj == nk - 1)
    def _():
        o = acc_ref[0] / l_ref[0] - lam_ref[0] * (acc_ref[1] / l_ref[1])
        ms = jnp.mean(o * o, axis=-1, keepdims=True)
        o = o * lax.rsqrt(ms + LN_EPS) * g_ref[...] * (1.0 - lambda_init)
        o_ref[0] = o.astype(o_ref.dtype)


def _diff_attn(proj, slopes, lam, g, lambda_init, tq=512, tk=512):
    B, S, _ = proj.shape
    tq, tk = min(tq, S), min(tk, S)
    nq, nk = S // tq, S // tk
    qb, kb, vb = OFF_QD // 128, OFF_KD // 128, OFF_VD // 128
    return pl.pallas_call(
        functools.partial(_diff_attn_kernel, tq=tq, tk=tk, nk=nk, lambda_init=lambda_init),
        grid=(B, DIFF_HEADS, nq, nk),
        in_specs=[pl.BlockSpec(memory_space=pltpu.SMEM),
                  pl.BlockSpec(memory_space=pltpu.SMEM),
                  pl.BlockSpec((1, tq, 128), lambda b, h, i, j: (b, i, qb + h)),
                  pl.BlockSpec((1, tk, 128), lambda b, h, i, j: (b, j, kb + h)),
                  pl.BlockSpec((1, tk, 128), lambda b, h, i, j: (b, j, vb + h)),
                  pl.BlockSpec((1, DIFF_DV), lambda b, h, i, j: (0, 0))],
        out_specs=pl.BlockSpec((1, tq, 128), lambda b, h, i, j: (b, i, h)),
        out_shape=jax.ShapeDtypeStruct((B, S, DIFF_V), BF16),
        scratch_shapes=[pltpu.VMEM((2, tq, 1), F32), pltpu.VMEM((2, tq, 1), F32),
                        pltpu.VMEM((2, tq, DIFF_DV), F32)],
        compiler_params=_cparams(("parallel", "parallel", "parallel", "arbitrary")),
    )(slopes, lam, proj, proj, proj, g)


def _dilated_kernel(slope_ref, q_ref, kp_ref, kc_ref, kn_ref, vp_ref, vc_ref, vn_ref, o_ref, lse_ref,
                    *, tl, half, length, group, dilation):
    i = pl.program_id(2)
    q = q_ref[0]
    kcat = jnp.concatenate([kp_ref[0], kc_ref[0], kn_ref[0]], axis=0)
    vcat = jnp.concatenate([vp_ref[0], vc_ref[0], vn_ref[0]], axis=0)
    lane = lax.broadcasted_iota(jnp.int32, q.shape, 1)
    qpos = i * tl + lax.broadcasted_iota(jnp.int32, (tl, 3 * tl), 0)
    kpos = (i - 1) * tl + lax.broadcasted_iota(jnp.int32, (tl, 3 * tl), 1)
    rel = jnp.abs(qpos - kpos)
    ok = (rel <= half) & (kpos >= 0) & (kpos < length)
    relf = rel.astype(F32)
    olane = lax.broadcasted_iota(jnp.int32, (tl, DIL_OUT), 1)
    o_acc = jnp.zeros((tl, DIL_OUT), F32)
    lse_acc = jnp.zeros((tl, DIL_OUT), F32)
    for h in range(DIL_HEADS_PER_GROUP):
        sel = (lane >= h * DIL_HD) & (lane < (h + 1) * DIL_HD)
        qz = jnp.where(sel, q, jnp.zeros_like(q))
        s = lax.dot_general(qz, kcat, (((1,), (1,)), ((), ())), preferred_element_type=F32)
        s = s - (slope_ref[group * DIL_HEADS_PER_GROUP + h] * dilation) * relf
        s = jnp.where(ok, s, NEG_INF)
        m = jnp.max(s, axis=-1, keepdims=True)
        p = jnp.exp(s - m)
        l = jnp.sum(p, axis=-1, keepdims=True)
        pv = jnp.dot(p.astype(BF16), vcat, preferred_element_type=F32) / l
        osel = (olane >= h * DIL_HD) & (olane < (h + 1) * DIL_HD)
        o_acc = jnp.where(osel, pv, o_acc)
        lse_acc = jnp.where(osel, m + jnp.log(l), lse_acc)
    o_ref[0] = o_acc.astype(o_ref.dtype)
    lse_ref[0] = lse_acc


def _dilated_group(proj, slopes, group, tl=128):
    B, S, _ = proj.shape
    window, dil = DIL_CONFIGS[group]
    L = S // dil
    half = window // (2 * dil)
    tl = min(tl, L)
    assert half <= tl and L % tl == 0
    nb = L // tl
    pv = proj.reshape(B, L, dil * D_IN)
    nc = D_IN // DIL_OUT
    qc, kc, vc = OFF_QL // DIL_OUT + group, OFF_KL // DIL_OUT + group, OFF_VL // DIL_OUT + group

    def spec(col, shift):
        def imap(b, r, i):
            return (b, jnp.clip(i + shift, 0, nb - 1), r * nc + col)
        return pl.BlockSpec((1, tl, DIL_OUT), imap)

    out_spec = pl.BlockSpec((1, tl, DIL_OUT), lambda b, r, i: (b, i, r))
    o, lse = pl.pallas_call(
        functools.partial(_dilated_kernel, tl=tl, half=half, length=L, group=group, dilation=float(dil)),
        grid=(B, dil, nb),
        in_specs=[pl.BlockSpec(memory_space=pltpu.SMEM),
                  spec(qc, 0), spec(kc, -1), spec(kc, 0), spec(kc, 1),
                  spec(vc, -1), spec(vc, 0), spec(vc, 1)],
        out_specs=[out_spec, out_spec],
        out_shape=[jax.ShapeDtypeStruct((B, L, dil * DIL_OUT), BF16),
                   jax.ShapeDtypeStruct((B, L, dil * DIL_OUT), F32)],
        compiler_params=_cparams(("parallel", "parallel", "parallel")),
    )(slopes, pv, pv, pv, pv, pv, pv, pv)
    return o.reshape(B, S, DIL_OUT), lse.reshape(B, S, DIL_OUT)


def _mem_kv_kernel(m_ref, g_ref, b_ref, w_ref, o_ref):
    mn = _ln(m_ref[...], g_ref[...], b_ref[...]).astype(BF16)
    o_ref[...] = jnp.dot(mn, w_ref[...], preferred_element_type=F32).astype(o_ref.dtype)


def _mem_kv(mem2d, g, b, w, tm=256):
    T, D = mem2d.shape
    N = w.shape[1]
    return pl.pallas_call(
        _mem_kv_kernel,
        grid=(T // tm,),
        in_specs=[pl.BlockSpec((tm, D), lambda i: (i, 0)),
                  pl.BlockSpec((1, D), lambda i: (0, 0)),
                  pl.BlockSpec((1, D), lambda i: (0, 0)),
                  pl.BlockSpec((D, N), lambda i: (0, 0))],
        out_specs=pl.BlockSpec((tm, N), lambda i: (i, 0)),
        out_shape=jax.ShapeDtypeStruct((T, N), BF16),
        compiler_params=_cparams(("parallel",)),
    )(mem2d, g, b, w)


def _mem_attn_kernel(q_ref, kv_ref, o_ref):
    q = q_ref[0]
    k = kv_ref[0, :, :MEM_Q]
    v = kv_ref[0, :, MEM_Q:]
    lane = lax.broadcasted_iota(jnp.int32, q.shape, 1)
    o_acc = jnp.zeros(q.shape, F32)
    for h in range(MEM_HEADS):
        sel = (lane >= h * MEM_HD) & (lane < (h + 1) * MEM_HD)
        qz = jnp.where(sel, q, jnp.zeros_like(q))
        s = lax.dot_general(qz, k, (((1,), (1,)), ((), ())), preferred_element_type=F32)
        m = jnp.max(s, axis=-1, keepdims=True)
        p = jnp.exp(s - m)
        l = jnp.sum(p, axis=-1, keepdims=True)
        pv = jnp.dot(p.astype(BF16), v, preferred_element_type=F32) / l
        o_acc = jnp.where(sel, pv, o_acc)
    o_ref[0] = o_acc.astype(o_ref.dtype)


def _mem_attn(proj, kv, tm=512):
    B, S, _ = proj.shape
    M = kv.shape[1]
    tm = min(tm, S)
    qc = OFF_QM // MEM_Q
    return pl.pallas_call(
        _mem_attn_kernel,
        grid=(B, S // tm),
        in_specs=[pl.BlockSpec((1, tm, MEM_Q), lambda b, i: (b, i, qc)),
                  pl.BlockSpec((1, M, 2 * MEM_Q), lambda b, i: (b, 0, 0))],
        out_specs=pl.BlockSpec((1, tm, MEM_Q), lambda b, i: (b, i, 0)),
        out_shape=jax.ShapeDtypeStruct((B, S, MEM_Q), BF16),
        compiler_params=_cparams(("parallel", "parallel")),
    )(proj, kv)


def _merge_kernel(x_ref, od_ref, o0_ref, o1_ref, o2_ref, l0_ref, l1_ref, l2_ref, om_ref,
                  lng_ref, lnb_ref, wg_ref, bg_ref, wbd_ref, wbl_ref, wbm_ref, wo_ref, g1_ref, b1_ref,
                  x1_ref, x1b_ref):
    D = x_ref.shape[1]
    xn = _ln(x_ref[...], lng_ref[...], lnb_ref[...])
    xb = xn.astype(BF16)
    l0, l1, l2 = l0_ref[...], l1_ref[...], l2_ref[...]
    mx = jnp.maximum(jnp.maximum(l0, l1), l2)
    e0, e1, e2 = jnp.exp(l0 - mx), jnp.exp(l1 - mx), jnp.exp(l2 - mx)
    ol = (e0 * o0_ref[...].astype(F32) + e1 * o1_ref[...].astype(F32) + e2 * o2_ref[...].astype(F32)) / (e0 + e1 + e2)
    branches = (jnp.dot(od_ref[...], wbd_ref[...], preferred_element_type=F32),
                jnp.dot(ol.astype(BF16), wbl_ref[...], preferred_element_type=F32),
                jnp.dot(om_ref[...], wbm_ref[...], preferred_element_type=F32))
    merged = jnp.zeros(xn.shape, F32)
    for n, br in enumerate(branches):
        cols = slice(n * D, (n + 1) * D)
        gate = jax.nn.sigmoid(jnp.dot(xb, wg_ref[:, cols], preferred_element_type=F32) + bg_ref[:, cols])
        merged = merged + gate * br
    y = DEEPNORM_ALPHA * xn + jnp.dot(merged.astype(BF16), wo_ref[...], preferred_element_type=F32)
    x1 = _ln(y, g1_ref[...], b1_ref[...])
    x1_ref[...] = x1
    x1b_ref[...] = x1.astype(BF16)


def _merge(x, od, odil, lses, om, lng, lnb, wg, bg, wbd, wbl, wbm, wo, g1, b1, tm=256):
    T, D = x.shape
    row = lambda w: pl.BlockSpec((tm, w), lambda i: (i, 0))
    full = lambda a: pl.BlockSpec(a.shape, lambda i: (0,) * a.ndim)
    params = (lng, lnb, wg, bg, wbd, wbl, wbm, wo, g1, b1)
    return pl.pallas_call(
        _merge_kernel,
        grid=(T // tm,),
        in_specs=[row(D), row(D)] + [row(DIL_OUT)] * 7 + [full(a) for a in params],
        out_specs=[row(D), row(D)],
        out_shape=[jax.ShapeDtypeStruct((T, D), F32), jax.ShapeDtypeStruct((T, D), BF16)],
        compiler_params=_cparams(("parallel",)),
    )(x, od, *odil, *lses, om, *params)


def _router_kernel(x_ref, w_ref, bias_ref, idx_ref, wt_ref):
    logits = jnp.dot(x_ref[...], w_ref[...], preferred_element_type=F32)
    scores = jax.nn.sigmoid(logits)
    sel = scores + bias_ref[...]
    tm, E = sel.shape
    lane = lax.broadcasted_iota(jnp.int32, (tm, E), 1)
    olane = lax.broadcasted_iota(jnp.int32, idx_ref.shape, 1)
    idx_out = jnp.zeros(idx_ref.shape, jnp.int32)
    w_out = jnp.zeros(wt_ref.shape, F32)
    for k in range(TOP_K):
        mx = jnp.max(sel, axis=-1, keepdims=True)
        idx = jnp.min(jnp.where(sel == mx, lane, E), axis=-1, keepdims=True)
        hit = lane == idx
        wk = jnp.sum(jnp.where(hit, scores, 0.0), axis=-1, keepdims=True)
        sel = jnp.where(hit, -jnp.inf, sel)
        idx_out = jnp.where(olane == k, idx, idx_out)
        w_out = jnp.where(olane == k, wk, w_out)
    tot = jnp.sum(w_out, axis=-1, keepdims=True)
    idx_ref[...] = idx_out
    wt_ref[...] = w_out / tot * ROUTED_SCALE


def _router(x1b, w, bias, tm=512):
    T, D = x1b.shape
    E = w.shape[1]
    return pl.pallas_call(
        _router_kernel,
        grid=(T // tm,),
        in_specs=[pl.BlockSpec((tm, D), lambda i: (i, 0)),
                  pl.BlockSpec((D, E), lambda i: (0, 0)),
                  pl.BlockSpec((1, E), lambda i: (0, 0))],
        out_specs=[pl.BlockSpec((tm, 128), lambda i: (i, 0)), pl.BlockSpec((tm, 128), lambda i: (i, 0))],
        out_shape=[jax.ShapeDtypeStruct((T, 128), jnp.int32), jax.ShapeDtypeStruct((T, 128), F32)],
        compiler_params=_cparams(("parallel",)),
    )(x1b, w, bias)


def _expert_kernel(be_ref, rows_ref, wg_ref, wu_ref, wd_ref, o_ref):
    del be_ref
    rows = rows_ref[...]
    g = jnp.dot(rows, wg_ref[0], preferred_element_type=F32)
    u = jnp.dot(rows, wu_ref[0], preferred_element_type=F32)
    hid = (g * jax.nn.sigmoid(g)) * u
    o_ref[...] = jnp.dot(hid.astype(BF16), wd_ref[0], preferred_element_type=F32).astype(o_ref.dtype)


def _expert_ffn(blk_expert, rows, wg, wu, wd):
    n_rows, D = rows.shape
    F = wg.shape[2]
    nb = n_rows // EXPERT_ROWS
    grid_spec = pltpu.PrefetchScalarGridSpec(
        num_scalar_prefetch=1,
        grid=(nb,),
        in_specs=[pl.BlockSpec((EXPERT_ROWS, D), lambda i, be: (i, 0)),
                  pl.BlockSpec((1, D, F), lambda i, be: (be[i], 0, 0)),
                  pl.BlockSpec((1, D, F), lambda i, be: (be[i], 0, 0)),
                  pl.BlockSpec((1, F, D), lambda i, be: (be[i], 0, 0))],
        out_specs=pl.BlockSpec((EXPERT_ROWS, D), lambda i, be: (i, 0)),
    )
    return pl.pallas_call(
        _expert_kernel,
        grid_spec=grid_spec,
        out_shape=jax.ShapeDtypeStruct((n_rows, D), F32),
        compiler_params=_cparams(("arbitrary",)),
    )(blk_expert, rows, wg, wu, wd)


def _final_kernel(x1_ref, x1b_ref, r_ref, wg_ref, wu_ref, wd_ref, g_ref, b_ref, o_ref):
    xb = x1b_ref[...]
    g = jnp.dot(xb, wg_ref[...], preferred_element_type=F32)
    u = jnp.dot(xb, wu_ref[...], preferred_element_type=F32)
    hid = (g * jax.nn.sigmoid(g)) * u
    shared = jnp.dot(hid.astype(BF16), wd_ref[...], preferred_element_type=F32)
    y = DEEPNORM_ALPHA * x1_ref[...] + (r_ref[...] + shared)
    o_ref[...] = _ln(y, g_ref[...], b_ref[...])


def _final(x1, x1b, routed, wg, wu, wd, g, b, tm=512):
    T, D = x1.shape
    row = pl.BlockSpec((tm, D), lambda i: (i, 0))
    full = lambda a: pl.BlockSpec(a.shape, lambda i: (0,) * a.ndim)
    params = (wg, wu, wd, g, b)
    return pl.pallas_call(
        _final_kernel,
        grid=(T // tm,),
        in_specs=[row, row, row] + [full(a) for a in params],
        out_specs=row,
        out_shape=jax.ShapeDtypeStruct((T, D), F32),
        compiler_params=_cparams(("parallel",)),
    )(x1, x1b, routed, *params)


def _moe_routed(x1b, top_idx, top_w, wg, wu, wd):
    T, D = x1b.shape
    E = wg.shape[0]
    n_assign = T * TOP_K
    e_flat = top_idx.reshape(n_assign)
    t_flat = jnp.repeat(jnp.arange(T, dtype=jnp.int32), TOP_K)
    w_flat = top_w.reshape(n_assign)
    order = jnp.argsort(e_flat)
    e_s, t_s, w_s = e_flat[order], t_flat[order], w_flat[order]
    counts = jnp.bincount(e_flat, length=E)
    padded = (counts + EXPERT_ROWS - 1) // EXPERT_ROWS * EXPERT_ROWS
    start = jnp.cumsum(counts) - counts
    pend = jnp.cumsum(padded)
    pstart = pend - padded
    dest = pstart[e_s] + jnp.arange(n_assign, dtype=jnp.int32) - start[e_s]
    n_blocks = -(-n_assign // EXPERT_ROWS) + E
    n_rows = n_blocks * EXPERT_ROWS
    tok_buf = jnp.full((n_rows,), T, dtype=jnp.int32).at[dest].set(t_s)
    w_buf = jnp.zeros((n_rows,), F32).at[dest].set(w_s)
    blk_expert = jnp.minimum(
        jnp.searchsorted(pend, jnp.arange(n_blocks, dtype=jnp.int32) * EXPERT_ROWS, side='right'), E - 1
    ).astype(jnp.int32)
    x_pad = jnp.concatenate([x1b, jnp.zeros((1, D), x1b.dtype)], axis=0)
    rows = x_pad[tok_buf]
    ys = _expert_ffn(blk_expert, rows, wg, wu, wd) * w_buf[:, None]
    return jnp.zeros((T + 1, D), F32).at[tok_buf].add(ys)[:T]


def _trunk(x, mem, p):
    B, S, D = x.shape
    T = B * S
    x2 = x.reshape(T, D)
    proj = _ln_proj(x2, p['ln_in_g'], p['ln_in_b'], p['w_in']).reshape(B, S, D_IN)
    od = _diff_attn(proj, p['diff_slopes'], p['lam'], p['diff_subln_g'], p['lambda_init'])
    dil = [_dilated_group(proj, p['dil_slopes'], g) for g in range(len(DIL_CONFIGS))]
    kv = _mem_kv(mem.reshape(-1, D), p['ln_mem_g'], p['ln_mem_b'], p['w_mem_kv']).reshape(B, mem.shape[1], 2 * MEM_Q)
    om = _mem_attn(proj, kv)
    x1, x1b = _merge(x2, od.reshape(T, DIFF_V), [o.reshape(T, DIL_OUT) for o, _ in dil],
                     [l.reshape(T, DIL_OUT) for _, l in dil], om.reshape(T, MEM_Q),
                     p['ln_in_g'], p['ln_in_b'], p['w_gate'], p['b_gate'], p['w_br_diff'], p['w_br_dil'],
                     p['w_br_mem'], p['w_out'], p['ln1_g'], p['ln1_b'])
    idx, wt = _router(x1b, p['w_router'], p['router_bias'])
    routed = _moe_routed(x1b, idx[:, :TOP_K], wt[:, :TOP_K], p['w_e_gate'], p['w_e_up'], p['w_e_down'])
    y = _final(x1, x1b, routed, p['w_s_gate'], p['w_s_up'], p['w_s_down'], p['ln2_g'], p['ln2_b'])
    return y.reshape(B, S, D)


def kernel(x_prompt, x_sample, mem_prompt, mem_sample, ln_in_g, ln_in_b, ln_mem_g, ln_mem_b, w_in, w_gate, b_gate,
           lambda_q1, lambda_k1, lambda_q2, lambda_k2, diff_subln_g, w_br_diff, w_br_dil, w_br_mem, w_mem_kv,
           w_out, ln1_g, ln1_b, w_router, router_bias, w_e_gate, w_e_up, w_e_down, w_s_gate, w_s_up, w_s_down,
           ln2_g, ln2_b):
    l = 0
    lambda_init = 0.8 - 0.6 * math.exp(-0.3 * l)
    lam = (jnp.exp(jnp.sum(lambda_q1[l] * lambda_k1[l])) - jnp.exp(jnp.sum(lambda_q2[l] * lambda_k2[l]))
           + lambda_init).astype(F32).reshape(1)
    colscale = np.ones((D_IN,), np.float32)
    for off, width, hd in ((OFF_QD, DIFF_QK, DIFF_DK), (OFF_QL, DIL_W, DIL_HD), (OFF_QM, MEM_Q, MEM_HD)):
        colscale[off:off + width] = hd ** -0.5
    row = lambda a: a.reshape(1, -1).astype(F32)
    p = dict(
        ln_in_g=row(ln_in_g), ln_in_b=row(ln_in_b), ln_mem_g=row(ln_mem_g[l]), ln_mem_b=row(ln_mem_b[l]),
        w_in=(w_in[l] * colscale).astype(BF16), w_gate=w_gate[l].astype(BF16), b_gate=row(b_gate[l]),
        lam=lam, lambda_init=lambda_init, diff_subln_g=row(diff_subln_g[l]),
        diff_slopes=jnp.asarray(_alibi_slopes(DIFF_HEADS)), dil_slopes=jnp.asarray(_alibi_slopes(N_DIL_HEADS)),
        w_br_diff=w_br_diff[l].astype(BF16), w_br_dil=w_br_dil[l].astype(BF16), w_br_mem=w_br_mem[l].astype(BF16),
        w_mem_kv=w_mem_kv[l].astype(BF16), w_out=w_out[l].astype(BF16), ln1_g=row(ln1_g[l]), ln1_b=row(ln1_b[l]),
        w_router=w_router[l].astype(BF16), router_bias=row(router_bias[l]),
        w_e_gate=w_e_gate[l].astype(BF16), w_e_up=w_e_up[l].astype(BF16), w_e_down=w_e_down[l].astype(BF16),
        w_s_gate=w_s_gate[l].astype(BF16), w_s_up=w_s_up[l].astype(BF16), w_s_down=w_s_down[l].astype(BF16),
        ln2_g=row(ln2_g[l]), ln2_b=row(ln2_b[l]),
    )
    return (_trunk(x_prompt, mem_prompt, p), _trunk(x_sample, mem_sample, p))
```

```python
import functools
import math

import numpy as np
import jax
import jax.numpy as jnp
from jax import lax
from jax.experimental import pallas as pl
from jax.experimental.pallas import tpu as pltpu

F32 = jnp.float32
BF16 = jnp.bfloat16

DIFF_HEADS = 8
DIFF_DK = 64
DIFF_DV = 128
DIL_CONFIGS = ((128, 1), (512, 4), (2048, 16))
DIL_HEADS_PER_GROUP = 4
DIL_HD = 64
MEM_HEADS = 4
MEM_HD = 64
TOP_K = 8
ROUTED_SCALE = 2.5
LN_EPS = 1e-5
NEG_INF = -1e30
DEPTH = 1
DEEPNORM_ALPHA = (2 * DEPTH) ** 0.25
LOG2E = math.log2(math.e)

D_MODEL = 1024
DIFF_QK = DIFF_HEADS * 2 * DIFF_DK
DIFF_V = DIFF_HEADS * DIFF_DV
N_DIL_HEADS = len(DIL_CONFIGS) * DIL_HEADS_PER_GROUP
DIL_W = N_DIL_HEADS * DIL_HD
DIL_OUT = DIL_HEADS_PER_GROUP * DIL_HD
MEM_Q = MEM_HEADS * MEM_HD
D_IN = 2 * DIFF_QK + DIFF_V + 3 * DIL_W + MEM_Q
SRC_VD = 2 * DIFF_QK
D_TOK = D_IN - DIFF_V
OFF_QD, OFF_KD = 0, DIFF_QK
OFF_QL = 2 * DIFF_QK
OFF_KL = OFF_QL + DIL_W
OFF_VL = OFF_KL + DIL_W
OFF_QM = OFF_VL + DIL_W
SEQ_TILE = 512
POS_SPLIT = 64
ONES_ROWS = 16

VMEM_LIMIT = 56 * 1024 * 1024
EXPERT_ROWS = 128
ROW_GROUP = 8
MOE_TILE = 4096


def _cparams(sem):
    return pltpu.CompilerParams(dimension_semantics=sem, vmem_limit_bytes=VMEM_LIMIT)


def _ln(x, g, b):
    mu = jnp.mean(x, axis=-1, keepdims=True)
    xc = x - mu
    var = jnp.mean(xc * xc, axis=-1, keepdims=True)
    return xc * lax.rsqrt(var + LN_EPS) * g + b


def _alibi_slopes(n):
    return np.asarray(2.0 ** (-8.0 * np.arange(1, n + 1) / n), dtype=np.float32)


def _ln_proj_kernel(x_ref, g_ref, b_ref, w_ref, wvt_ref, o_ref, vt_ref, xn_ref, *, tn):
    xn_ref[...] = _ln(x_ref[...], g_ref[...], b_ref[...]).astype(BF16)
    for n in range(o_ref.shape[1] // tn):
        cols = slice(n * tn, (n + 1) * tn)
        o_ref[:, cols] = jnp.dot(xn_ref[...], w_ref[:, cols], preferred_element_type=F32).astype(o_ref.dtype)
    vt_ref[0] = lax.dot_general(wvt_ref[...], xn_ref[...], (((1,), (1,)), ((), ())),
                                preferred_element_type=F32).astype(vt_ref.dtype)


def _ln_proj(x, g, b, w, wvt, tm=SEQ_TILE, tn=512):
    T, D = x.shape
    N = w.shape[1]
    NV = wvt.shape[0]
    return pl.pallas_call(
        functools.partial(_ln_proj_kernel, tn=tn),
        name="ln_proj",
        grid=(T // tm,),
        in_specs=[pl.BlockSpec((tm, D), lambda i: (i, 0)),
                  pl.BlockSpec((1, D), lambda i: (0, 0)),
                  pl.BlockSpec((1, D), lambda i: (0, 0)),
                  pl.BlockSpec((D, N), lambda i: (0, 0)),
                  pl.BlockSpec((NV, D), lambda i: (0, 0))],
        out_specs=[pl.BlockSpec((tm, N), lambda i: (i, 0)),
                   pl.BlockSpec((1, NV, tm), lambda i: (i, 0, 0))],
        out_shape=[jax.ShapeDtypeStruct((T, N), BF16), jax.ShapeDtypeStruct((T // tm, NV, tm), BF16)],
        scratch_shapes=[pltpu.VMEM((tm, D), BF16)],
        compiler_params=_cparams(("parallel",)),
    )(x, g, b, w, wvt)


def _split3_bf16(x):
    hi = x.astype(BF16).astype(F32)
    mid = (x - hi).astype(BF16).astype(F32)
    return hi, mid, x - hi - mid


def _diff_attn_kernel(slope_ref, lam_ref, q_ref, k_ref, kpos_ref, vt_ref, g_ref, o_ref,
                      qx_ref, m_ref, acc_ref, sa_ref, sb_ref, *, tq, tk, nk, lambda_init):
    h = pl.program_id(1)
    qi = pl.program_id(2)
    slope = slope_ref[h]
    q = q_ref[0]
    lane = lax.broadcasted_iota(jnp.int32, q.shape, 1)
    ipos = (qi * tq + lax.broadcasted_iota(jnp.int32, q.shape, 0)).astype(F32)
    a_hi, a_mid, a_lo = _split3_bf16(-slope * ipos)
    c = jnp.full(q.shape, slope, F32)
    c_hi = c.astype(BF16).astype(F32)
    c_lo = c - c_hi
    feats = (a_hi, a_mid, a_lo, POS_SPLIT * c_hi, c_hi, POS_SPLIT * c_lo, c_lo)
    ext = jnp.zeros(q.shape, F32)
    for n, f in enumerate(feats):
        ext = jnp.where(lane == n, f, ext)
    for comp in range(2):
        qz = jnp.where((lane >= comp * DIFF_DK) & (lane < (comp + 1) * DIFF_DK), q, jnp.zeros_like(q))
        qx_ref[comp, 0] = jnp.concatenate([qz, ext.astype(BF16)], axis=1)
        qx_ref[comp, 1] = jnp.concatenate([qz, (-ext).astype(BF16)], axis=1)
        qx_ref[comp, 2] = jnp.concatenate([qz, jnp.zeros_like(qz)], axis=1)
    m_ref[...] = jnp.full(m_ref.shape, NEG_INF, F32)
    acc_ref[...] = jnp.zeros(acc_ref.shape, F32)
    ones = jnp.where(lax.broadcasted_iota(jnp.int32, (ONES_ROWS, tk), 0) == 0, 1.0, 0.0).astype(BF16)

    def scores(j, side, comp):
        kx = jnp.concatenate([k_ref[0, j], kpos_ref[j]], axis=1)
        return lax.dot_general(kx, qx_ref[comp, side], (((1,), (1,)), ((), ())), preferred_element_type=F32)

    def softmax_pv(st, j, comp):
        vx = jnp.concatenate([vt_ref[0, j], ones], axis=0)
        m_old = m_ref[comp]
        m_new = jnp.maximum(m_old, jnp.max(st, axis=0, keepdims=True))
        pt = jnp.exp2(st - m_new).astype(BF16)
        acc_ref[comp] = jnp.exp2(m_old - m_new) * acc_ref[comp] + jnp.dot(vx, pt, preferred_element_type=F32)
        m_ref[comp] = m_new

    def key_block(b):
        side = (b - 1 >= qi).astype(jnp.int32)
        return b - 1 + side, side

    def fill(b, dst_ref, overlap=False):
        if overlap:
            jpos = qi * tk + lax.broadcasted_iota(jnp.int32, (tk, tq), 0)
            ipos_t = qi * tq + lax.broadcasted_iota(jnp.int32, (tk, tq), 1)
            bias = slope * jnp.abs(jpos - ipos_t).astype(F32)
            for comp in range(2):
                dst_ref[comp] = scores(qi, 2, comp) - bias
        else:
            j, side = key_block(b)
            for comp in range(2):
                dst_ref[comp] = scores(j, side, comp)

    def drain(b, src_ref, overlap=False):
        j = qi if overlap else key_block(b)[0]
        for comp in range(2):
            softmax_pv(src_ref[comp], j, comp)

    def trip(t, first=False, last=False):
        fill(2 * t + 1, sb_ref)
        drain(2 * t, sa_ref, overlap=first)
        if not last:
            fill(2 * t + 2, sa_ref)
        drain(2 * t + 1, sb_ref)

    def middle(t, carry):
        trip(t)
        return carry

    fill(0, sa_ref, overlap=True)
    if nk == 1:
        drain(0, sa_ref, overlap=True)
    else:
        n_trips = nk // 2
        trip(0, first=True, last=n_trips == 1)
        if n_trips > 2:
            lax.fori_loop(1, n_trips - 1, middle, 0)
        if n_trips > 1:
            trip(n_trips - 1, last=True)

    o0 = acc_ref[0, :DIFF_DV] / acc_ref[0, DIFF_DV:DIFF_DV + 1]
    o1 = acc_ref[1, :DIFF_DV] / acc_ref[1, DIFF_DV:DIFF_DV + 1]
    o = o0 - lam_ref[0] * o1
    ms = jnp.mean(o * o, axis=0, keepdims=True)
    o = o * lax.rsqrt(ms + LN_EPS) * g_ref[...] * (1.0 - lambda_init)
    o_ref[0] = o.T.astype(o_ref.dtype)


def _diff_attn(proj, vt, slopes, lam, g_col, lambda_init):
    B, S, _ = proj.shape
    tq = tk = min(SEQ_TILE, S)
    nk = S // tk
    assert nk == 1 or nk % 2 == 0, "the key-block pipeline handles blocks in pairs"
    pos = lax.broadcasted_iota(jnp.int32, (nk, tk, 128), 0) * tk + lax.broadcasted_iota(jnp.int32, (nk, tk, 128), 1)
    lane = lax.broadcasted_iota(jnp.int32, (nk, tk, 128), 2)
    kpos = jnp.where(lane < 3, 1, jnp.where((lane == 3) | (lane == 5), pos // POS_SPLIT,
                                            jnp.where((lane == 4) | (lane == 6), pos % POS_SPLIT, 0))).astype(BF16)
    qb, kb = OFF_QD // 128, OFF_KD // 128
    return pl.pallas_call(
        functools.partial(_diff_attn_kernel, tq=tq, tk=tk, nk=nk, lambda_init=lambda_init),
        name="diff_attn",
        grid=(B, DIFF_HEADS, S // tq),
        in_specs=[pl.BlockSpec(memory_space=pltpu.SMEM),
                  pl.BlockSpec(memory_space=pltpu.SMEM),
                  pl.BlockSpec((1, tq, 128), lambda b, h, i: (b, i, qb + h)),
                  pl.BlockSpec((1, nk, tk, 128), lambda b, h, i: (b, 0, 0, kb + h)),
                  pl.BlockSpec((nk, tk, 128), lambda b, h, i: (0, 0, 0)),
                  pl.BlockSpec((1, nk, DIFF_DV, tk), lambda b, h, i: (b, 0, h, 0)),
                  pl.BlockSpec((DIFF_DV, 1), lambda b, h, i: (0, 0))],
        out_specs=pl.BlockSpec((1, tq, 128), lambda b, h, i: (b, i, h)),
        out_shape=jax.ShapeDtypeStruct((B, S, DIFF_V), BF16),
        scratch_shapes=[pltpu.VMEM((2, 3, tq, 256), BF16), pltpu.VMEM((2, 1, tq), F32),
                        pltpu.VMEM((2, DIFF_DV + ONES_ROWS, tq), F32),
                        pltpu.VMEM((2, tk, tq), F32), pltpu.VMEM((2, tk, tq), F32)],
        compiler_params=_cparams(("parallel", "parallel", "parallel")),
    )(slopes, lam, proj, proj.reshape(B, nk, tk, -1), kpos, vt.reshape(B, nk, DIFF_V, tk), g_col)


def _dilated_kernel(slope_ref, q_ref, kp_ref, kc_ref, kn_ref, vp_ref, vc_ref, vn_ref, o_ref, lse_ref,
                    *, tl, half, length, group, dilation):
    i = pl.program_id(2)
    q = q_ref[0]
    kcat = jnp.concatenate([kp_ref[0], kc_ref[0], kn_ref[0]], axis=0)
    vcat = jnp.concatenate([vp_ref[0], vc_ref[0], vn_ref[0]], axis=0)
    lane = lax.broadcasted_iota(jnp.int32, q.shape, 1)
    qpos = i * tl + lax.broadcasted_iota(jnp.int32, (tl, 3 * tl), 0)
    kpos = (i - 1) * tl + lax.broadcasted_iota(jnp.int32, (tl, 3 * tl), 1)
    rel = jnp.abs(qpos - kpos)
    ok = (rel <= half) & (kpos >= 0) & (kpos < length)
    relf = rel.astype(F32)
    olane = lax.broadcasted_iota(jnp.int32, (tl, DIL_OUT), 1)
    o_acc = jnp.zeros((tl, DIL_OUT), F32)
    lse_acc = jnp.zeros((tl, DIL_OUT), F32)
    for h in range(DIL_HEADS_PER_GROUP):
        sel = (lane >= h * DIL_HD) & (lane < (h + 1) * DIL_HD)
        qz = jnp.where(sel, q, jnp.zeros_like(q))
        s = lax.dot_general(qz, kcat, (((1,), (1,)), ((), ())), preferred_element_type=F32)
        s = s - (slope_ref[group * DIL_HEADS_PER_GROUP + h] * dilation) * relf
        s = jnp.where(ok, s, NEG_INF)
        m = jnp.max(s, axis=-1, keepdims=True)
        p = jnp.exp(s - m)
        l = jnp.sum(p, axis=-1, keepdims=True)
        pv = jnp.dot(p.astype(BF16), vcat, preferred_element_type=F32) / l
        osel = (olane >= h * DIL_HD) & (olane < (h + 1) * DIL_HD)
        o_acc = jnp.where(osel, pv, o_acc)
        lse_acc = jnp.where(osel, m + jnp.log(l), lse_acc)
    o_ref[0] = o_acc.astype(o_ref.dtype)
    lse_ref[0] = lse_acc


def _dilated_group(proj, slopes, group, tl=128):
    B, S, _ = proj.shape
    window, dil = DIL_CONFIGS[group]
    L = S // dil
    half = window // (2 * dil)
    tl = min(tl, L)
    assert half <= tl and L % tl == 0
    nb = L // tl
    pv = proj.reshape(B, L, dil * D_TOK)
    nc = D_TOK // DIL_OUT
    qc, kc, vc = OFF_QL // DIL_OUT + group, OFF_KL // DIL_OUT + group, OFF_VL // DIL_OUT + group

    def spec(col, shift):
        def imap(b, r, i):
            return (b, jnp.clip(i + shift, 0, nb - 1), r * nc + col)
        return pl.BlockSpec((1, tl, DIL_OUT), imap)

    out_spec = pl.BlockSpec((1, tl, DIL_OUT), lambda b, r, i: (b, i, r))
    o, lse = pl.pallas_call(
        functools.partial(_dilated_kernel, tl=tl, half=half, length=L, group=group, dilation=float(dil)),
        name=f"dilated_attn_{group}",
        grid=(B, dil, nb),
        in_specs=[pl.BlockSpec(memory_space=pltpu.SMEM),
                  spec(qc, 0), spec(kc, -1), spec(kc, 0), spec(kc, 1),
                  spec(vc, -1), spec(vc, 0), spec(vc, 1)],
        out_specs=[out_spec, out_spec],
        out_shape=[jax.ShapeDtypeStruct((B, L, dil * DIL_OUT), BF16),
                   jax.ShapeDtypeStruct((B, L, dil * DIL_OUT), F32)],
        compiler_params=_cparams(("parallel", "parallel", "parallel")),
    )(slopes, pv, pv, pv, pv, pv, pv, pv)
    return o.reshape(B, S, DIL_OUT), lse.reshape(B, S, DIL_OUT)


def _mem_kv_kernel(m_ref, g_ref, b_ref, w_ref, o_ref):
    mn = _ln(m_ref[...], g_ref[...], b_ref[...]).astype(BF16)
    o_ref[...] = jnp.dot(mn, w_ref[...], preferred_element_type=F32).astype(o_ref.dtype)


def _mem_kv(mem2d, g, b, w, tm=256):
    T, D = mem2d.shape
    N = w.shape[1]
    return pl.pallas_call(
        _mem_kv_kernel,
        name="mem_kv",
        grid=(T // tm,),
        in_specs=[pl.BlockSpec((tm, D), lambda i: (i, 0)),
                  pl.BlockSpec((1, D), lambda i: (0, 0)),
                  pl.BlockSpec((1, D), lambda i: (0, 0)),
                  pl.BlockSpec((D, N), lambda i: (0, 0))],
        out_specs=pl.BlockSpec((tm, N), lambda i: (i, 0)),
        out_shape=jax.ShapeDtypeStruct((T, N), BF16),
        compiler_params=_cparams(("parallel",)),
    )(mem2d, g, b, w)


def _mem_attn_kernel(q_ref, kv_ref, o_ref):
    q = q_ref[0]
    k = kv_ref[0, :, :MEM_Q]
    v = kv_ref[0, :, MEM_Q:]
    lane = lax.broadcasted_iota(jnp.int32, q.shape, 1)
    o_acc = jnp.zeros(q.shape, F32)
    for h in range(MEM_HEADS):
        sel = (lane >= h * MEM_HD) & (lane < (h + 1) * MEM_HD)
        qz = jnp.where(sel, q, jnp.zeros_like(q))
        s = lax.dot_general(qz, k, (((1,), (1,)), ((), ())), preferred_element_type=F32)
        m = jnp.max(s, axis=-1, keepdims=True)
        p = jnp.exp(s - m)
        l = jnp.sum(p, axis=-1, keepdims=True)
        pv = jnp.dot(p.astype(BF16), v, preferred_element_type=F32) / l
        o_acc = jnp.where(sel, pv, o_acc)
    o_ref[0] = o_acc.astype(o_ref.dtype)


def _mem_attn(proj, kv, tm=512):
    B, S, _ = proj.shape
    M = kv.shape[1]
    tm = min(tm, S)
    qc = OFF_QM // MEM_Q
    return pl.pallas_call(
        _mem_attn_kernel,
        name="mem_attn",
        grid=(B, S // tm),
        in_specs=[pl.BlockSpec((1, tm, MEM_Q), lambda b, i: (b, i, qc)),
                  pl.BlockSpec((1, M, 2 * MEM_Q), lambda b, i: (b, 0, 0))],
        out_specs=pl.BlockSpec((1, tm, MEM_Q), lambda b, i: (b, i, 0)),
        out_shape=jax.ShapeDtypeStruct((B, S, MEM_Q), BF16),
        compiler_params=_cparams(("parallel", "parallel")),
    )(proj, kv)


HI16 = -65536


def _pack_halves(x):
    half = x.shape[1] // 2
    lo = lax.bitcast_convert_type(x[:, :half].astype(BF16).astype(F32), jnp.int32)
    hi = lax.bitcast_convert_type(x[:, half:].astype(BF16).astype(F32), jnp.int32)
    return lax.shift_right_logical(lo, 16) | (hi & HI16)


def _unpack_halves(r):
    lo = lax.bitcast_convert_type(lax.shift_left(r, 16), F32).astype(BF16)
    hi = lax.bitcast_convert_type(r & HI16, F32).astype(BF16)
    return lo, hi


def _merge_kernel(x_ref, od_ref, o0_ref, o1_ref, o2_ref, l0_ref, l1_ref, l2_ref, om_ref,
                  lng_ref, lnb_ref, wg_ref, bg_ref, wbd_ref, wbl_ref, wbm_ref, wo_ref, g1_ref, b1_ref,
                  x1_ref, xpk_ref):
    D = x_ref.shape[1]
    xn = _ln(x_ref[...], lng_ref[...], lnb_ref[...])
    xb = xn.astype(BF16)
    l0, l1, l2 = l0_ref[...], l1_ref[...], l2_ref[...]
    mx = jnp.maximum(jnp.maximum(l0, l1), l2)
    e0, e1, e2 = jnp.exp(l0 - mx), jnp.exp(l1 - mx), jnp.exp(l2 - mx)
    ol = (e0 * o0_ref[...].astype(F32) + e1 * o1_ref[...].astype(F32) + e2 * o2_ref[...].astype(F32)) / (e0 + e1 + e2)
    branches = (jnp.dot(od_ref[...], wbd_ref[...], preferred_element_type=F32),
                jnp.dot(ol.astype(BF16), wbl_ref[...], preferred_element_type=F32),
                jnp.dot(om_ref[...], wbm_ref[...], preferred_element_type=F32))
    merged = jnp.zeros(xn.shape, F32)
    for n, br in enumerate(branches):
        cols = slice(n * D, (n + 1) * D)
        gate = jax.nn.sigmoid(jnp.dot(xb, wg_ref[:, cols], preferred_element_type=F32) + bg_ref[:, cols])
        merged = merged + gate * br
    y = DEEPNORM_ALPHA * xn + jnp.dot(merged.astype(BF16), wo_ref[...], preferred_element_type=F32)
    x1 = _ln(y, g1_ref[...], b1_ref[...])
    x1_ref[...] = x1
    xpk_ref[...] = _pack_halves(x1)


def _merge(x, od, odil, lses, om, lng, lnb, wg, bg, wbd, wbl, wbm, wo, g1, b1, tm=256):
    T, D = x.shape
    row = lambda w: pl.BlockSpec((tm, w), lambda i: (i, 0))
    full = lambda a: pl.BlockSpec(a.shape, lambda i: (0,) * a.ndim)
    params = (lng, lnb, wg, bg, wbd, wbl, wbm, wo, g1, b1)
    return pl.pallas_call(
        _merge_kernel,
        name="merge_out",
        grid=(T // tm,),
        in_specs=[row(D), row(D)] + [row(DIL_OUT)] * 7 + [full(a) for a in params],
        out_specs=[row(D), row(D // 2)],
        out_shape=[jax.ShapeDtypeStruct((T, D), F32), jax.ShapeDtypeStruct((T, D // 2), jnp.int32)],
        compiler_params=_cparams(("parallel",)),
    )(x, od, *odil, *lses, om, *params)


def _router_kernel(x_ref, w_ref, bias_ref, idx_ref, wd_ref):
    logits = jnp.dot(x_ref[...].astype(BF16), w_ref[...], preferred_element_type=F32)
    scores = jax.nn.sigmoid(logits)
    sel = scores + bias_ref[...]
    tm, E = sel.shape
    lane = lax.broadcasted_iota(jnp.int32, (tm, E), 1)
    olane = lax.broadcasted_iota(jnp.int32, idx_ref.shape, 1)
    idx_out = jnp.zeros(idx_ref.shape, jnp.int32)
    chosen = jnp.zeros((tm, E), jnp.bool_)
    for k in range(TOP_K):
        mx = jnp.max(sel, axis=-1, keepdims=True)
        idx = jnp.min(jnp.where(sel == mx, lane, E), axis=-1, keepdims=True)
        hit = lane == idx
        chosen = chosen | hit
        sel = jnp.where(hit, -jnp.inf, sel)
        idx_out = jnp.where(olane == k, idx, idx_out)
    picked = jnp.where(chosen, scores, 0.0)
    idx_ref[...] = idx_out
    wd_ref[...] = picked / jnp.sum(picked, axis=-1, keepdims=True) * ROUTED_SCALE


def _router(x1, w, bias, tm=512):
    T, D = x1.shape
    E = w.shape[1]
    return pl.pallas_call(
        _router_kernel,
        name="router",
        grid=(T // tm,),
        in_specs=[pl.BlockSpec((tm, D), lambda i: (i, 0)),
                  pl.BlockSpec((D, E), lambda i: (0, 0)),
                  pl.BlockSpec((1, E), lambda i: (0, 0))],
        out_specs=[pl.BlockSpec((tm, 128), lambda i: (i, 0)), pl.BlockSpec((tm, E), lambda i: (i, 0))],
        out_shape=[jax.ShapeDtypeStruct((T, 128), jnp.int32), jax.ShapeDtypeStruct((T, E), F32)],
        compiler_params=_cparams(("parallel",)),
    )(x1, w, bias)


def _moe_kernel(offs_ref, tok_ref, xpk_ref, rw_ref, wg_ref, wu_ref, wd_ref, out_ref, rows_ref, wrow_ref, y_ref,
                *, chunk, n_experts):
    tile = pl.program_id(0)
    e = pl.program_id(1)

    @pl.when(e == 0)
    def _():
        out_ref[...] = jnp.zeros(out_ref.shape, F32)

    @pl.when((tile == 0) & (e == 0))
    def _():
        rows_ref[...] = jnp.zeros(rows_ref.shape, rows_ref.dtype)
        wrow_ref[...] = jnp.zeros(wrow_ref.shape, F32)

    start = offs_ref[tile * (n_experts + 1) + e]
    n = offs_ref[tile * (n_experts + 1) + e + 1] - start
    half = wg_ref.shape[1] // 2

    def one_chunk(c, carry):
        base = start + c * chunk
        cnt = jnp.minimum(chunk, n - c * chunk)

        def gather(g, carry):
            i0 = pl.multiple_of(g * ROW_GROUP, ROW_GROUP)
            for s in range(ROW_GROUP):
                t = tok_ref[0, 0, base + i0 + s]
                rows_ref[pl.ds(i0 + s, 1), :] = xpk_ref[pl.ds(t, 1), :]
                wrow_ref[pl.ds(i0 + s, 1), :] = rw_ref[pl.ds(t, 1), :]
            return carry

        lax.fori_loop(0, (cnt + ROW_GROUP - 1) // ROW_GROUP, gather, 0)
        lo, hi = _unpack_halves(rows_ref[...])
        g = (jnp.dot(lo, wg_ref[0, :half], preferred_element_type=F32)
             + jnp.dot(hi, wg_ref[0, half:], preferred_element_type=F32))
        u = (jnp.dot(lo, wu_ref[0, :half], preferred_element_type=F32)
             + jnp.dot(hi, wu_ref[0, half:], preferred_element_type=F32))
        lane = lax.broadcasted_iota(jnp.int32, wrow_ref.shape, 1)
        wcol = jnp.sum(jnp.where(lane == e, wrow_ref[...], 0.0), axis=1, keepdims=True)
        wcol = jnp.where(lax.broadcasted_iota(jnp.int32, wcol.shape, 0) < cnt, wcol, 0.0)
        hid = (g * jax.nn.sigmoid(g)) * u * wcol
        y_ref[...] = jnp.dot(hid.astype(BF16), wd_ref[0], preferred_element_type=F32)

        def scatter_group(g, carry):
            i0 = pl.multiple_of(g * ROW_GROUP, ROW_GROUP)
            toks = [tok_ref[0, 0, base + i0 + s] for s in range(ROW_GROUP)]
            yv = y_ref[pl.ds(i0, ROW_GROUP), :]
            olds = [out_ref[pl.ds(t, 1), :] for t in toks]
            for s in range(ROW_GROUP):
                out_ref[pl.ds(toks[s], 1), :] = olds[s] + yv[s:s + 1, :]
            return carry

        def scatter_row(i, carry):
            t = tok_ref[0, 0, base + i]
            out_ref[pl.ds(t, 1), :] += y_ref[pl.ds(i, 1), :]
            return carry

        n_groups = cnt // ROW_GROUP
        lax.fori_loop(0, n_groups, scatter_group, 0)
        lax.fori_loop(n_groups * ROW_GROUP, cnt, scatter_row, 0)
        return carry

    lax.fori_loop(0, (n + chunk - 1) // chunk, one_chunk, 0)


def _moe_experts(offs, tok, xpk, rw, wg, wu, wd, tile):
    T, half = xpk.shape
    E, D, F = wg.shape
    nt = T // tile
    grid_spec = pltpu.PrefetchScalarGridSpec(
        num_scalar_prefetch=1,
        grid=(nt, E),
        in_specs=[pl.BlockSpec((1, 1, tok.shape[2]), lambda t, e, offs: (t, 0, 0), memory_space=pltpu.SMEM),
                  pl.BlockSpec((tile, half), lambda t, e, offs: (t, 0), pipeline_mode=pl.Buffered(1)),
                  pl.BlockSpec((tile, E), lambda t, e, offs: (t, 0), pipeline_mode=pl.Buffered(1)),
                  pl.BlockSpec((1, D, F), lambda t, e, offs: (e, 0, 0)),
                  pl.BlockSpec((1, D, F), lambda t, e, offs: (e, 0, 0)),
                  pl.BlockSpec((1, F, D), lambda t, e, offs: (e, 0, 0))],
        out_specs=pl.BlockSpec((tile, D), lambda t, e, offs: (t, 0)),
        scratch_shapes=[pltpu.VMEM((EXPERT_ROWS, half), jnp.int32), pltpu.VMEM((EXPERT_ROWS, E), F32),
                        pltpu.VMEM((EXPERT_ROWS, D), F32)],
    )
    return pl.pallas_call(
        functools.partial(_moe_kernel, chunk=EXPERT_ROWS, n_experts=E),
        name="moe_experts",
        grid_spec=grid_spec,
        out_shape=jax.ShapeDtypeStruct((T, D), F32),
        compiler_params=_cparams(("arbitrary", "arbitrary")),
    )(offs, tok, xpk, rw, wg, wu, wd)


def _final_kernel(x1_ref, r_ref, wg_ref, wu_ref, wd_ref, g_ref, b_ref, o_ref):
    xb = x1_ref[...].astype(BF16)
    g = jnp.dot(xb, wg_ref[...], preferred_element_type=F32)
    u = jnp.dot(xb, wu_ref[...], preferred_element_type=F32)
    hid = (g * jax.nn.sigmoid(g)) * u
    shared = jnp.dot(hid.astype(BF16), wd_ref[...], preferred_element_type=F32)
    y = DEEPNORM_ALPHA * x1_ref[...] + (r_ref[...] + shared)
    o_ref[...] = _ln(y, g_ref[...], b_ref[...])


def _final(x1, routed, wg, wu, wd, g, b, tm=512):
    T, D = x1.shape
    row = pl.BlockSpec((tm, D), lambda i: (i, 0))
    full = lambda a: pl.BlockSpec(a.shape, lambda i: (0,) * a.ndim)
    params = (wg, wu, wd, g, b)
    return pl.pallas_call(
        _final_kernel,
        name="shared_final",
        grid=(T // tm,),
        in_specs=[row, row] + [full(a) for a in params],
        out_specs=row,
        out_shape=jax.ShapeDtypeStruct((T, D), F32),
        compiler_params=_cparams(("parallel",)),
    )(x1, routed, *params)


def _moe_routed(xpk, top_idx, rw, wg, wu, wd):
    T = xpk.shape[0]
    E = wg.shape[0]
    tile = min(MOE_TILE, T)
    nt = T // tile
    e_tile = top_idx.reshape(nt, tile * TOP_K)
    order = jnp.argsort(e_tile, axis=1)
    e_sorted = jnp.take_along_axis(e_tile, order, axis=1)
    tok = jnp.pad((order // TOP_K).astype(jnp.int32), ((0, 0), (0, ROW_GROUP))).reshape(nt, 1, -1)
    bounds = jnp.arange(E + 1, dtype=jnp.int32)
    offs = jax.vmap(lambda row: jnp.searchsorted(row, bounds, side='left'))(e_sorted).astype(jnp.int32)
    return _moe_experts(offs.reshape(-1), tok, xpk, rw, wg, wu, wd, tile)


def _trunk(x, mem, p):
    B, S, D = x.shape
    T = B * S
    x2 = x.reshape(T, D)
    proj, vt = _ln_proj(x2, p['ln_in_g'], p['ln_in_b'], p['w_in'], p['w_vt'])
    proj = proj.reshape(B, S, D_TOK)
    od = _diff_attn(proj, vt, p['diff_slopes'], p['lam'], p['diff_subln_g'], p['lambda_init'])
    dil = [_dilated_group(proj, p['dil_slopes'], g) for g in range(len(DIL_CONFIGS))]
    kv = _mem_kv(mem.reshape(-1, D), p['ln_mem_g'], p['ln_mem_b'], p['w_mem_kv']).reshape(B, mem.shape[1], 2 * MEM_Q)
    om = _mem_attn(proj, kv)
    x1, xpk = _merge(x2, od.reshape(T, DIFF_V), [o.reshape(T, DIL_OUT) for o, _ in dil],
                     [l.reshape(T, DIL_OUT) for _, l in dil], om.reshape(T, MEM_Q),
                     p['ln_in_g'], p['ln_in_b'], p['w_gate'], p['b_gate'], p['w_br_diff'], p['w_br_dil'],
                     p['w_br_mem'], p['w_out'], p['ln1_g'], p['ln1_b'])
    idx, rw = _router(x1, p['w_router'], p['router_bias'])
    routed = _moe_routed(xpk, idx[:, :TOP_K], rw, p['w_e_gate'], p['w_e_up'], p['w_e_down'])
    y = _final(x1, routed, p['w_s_gate'], p['w_s_up'], p['w_s_down'], p['ln2_g'], p['ln2_b'])
    return y.reshape(B, S, D)


def kernel(x_prompt, x_sample, mem_prompt, mem_sample, ln_in_g, ln_in_b, ln_mem_g, ln_mem_b, w_in, w_gate, b_gate,
           lambda_q1, lambda_k1, lambda_q2, lambda_k2, diff_subln_g, w_br_diff, w_br_dil, w_br_mem, w_mem_kv,
           w_out, ln1_g, ln1_b, w_router, router_bias, w_e_gate, w_e_up, w_e_down, w_s_gate, w_s_up, w_s_down,
           ln2_g, ln2_b):
    l = 0
    lambda_init = 0.8 - 0.6 * math.exp(-0.3 * l)
    lam = (jnp.exp(jnp.sum(lambda_q1[l] * lambda_k1[l])) - jnp.exp(jnp.sum(lambda_q2[l] * lambda_k2[l]))
           + lambda_init).astype(F32).reshape(1)
    colscale = np.ones((D_IN,), np.float32)
    src_ql = SRC_VD + DIFF_V
    for off, width, hd in ((0, DIFF_QK, DIFF_DK), (src_ql, DIL_W, DIL_HD), (D_IN - MEM_Q, MEM_Q, MEM_HD)):
        colscale[off:off + width] = hd ** -0.5
    colscale[:DIFF_QK] *= LOG2E
    w_all = (w_in[l] * colscale).astype(BF16)
    row = lambda a: a.reshape(1, -1).astype(F32)
    p = dict(
        ln_in_g=row(ln_in_g), ln_in_b=row(ln_in_b), ln_mem_g=row(ln_mem_g[l]), ln_mem_b=row(ln_mem_b[l]),
        w_in=jnp.concatenate([w_all[:, :SRC_VD], w_all[:, src_ql:]], axis=1), w_vt=w_all[:, SRC_VD:src_ql].T,
        w_gate=w_gate[l].astype(BF16), b_gate=row(b_gate[l]),
        lam=lam, lambda_init=lambda_init, diff_subln_g=diff_subln_g[l].reshape(-1, 1).astype(F32),
        diff_slopes=jnp.asarray(_alibi_slopes(DIFF_HEADS) * np.float32(LOG2E)), dil_slopes=jnp.asarray(_alibi_slopes(N_DIL_HEADS)),
        w_br_diff=w_br_diff[l].astype(BF16), w_br_dil=w_br_dil[l].astype(BF16), w_br_mem=w_br_mem[l].astype(BF16),
        w_mem_kv=w_mem_kv[l].astype(BF16), w_out=w_out[l].astype(BF16), ln1_g=row(ln1_g[l]), ln1_b=row(ln1_b[l]),
        w_router=w_router[l].astype(BF16), router_bias=row(router_bias[l]),
        w_e_gate=w_e_gate[l].astype(BF16), w_e_up=w_e_up[l].astype(BF16), w_e_down=w_e_down[l].astype(BF16),
        w_s_gate=w_s_gate[l].astype(BF16), w_s_up=w_s_up[l].astype(BF16), w_s_down=w_s_down[l].astype(BF16),
        ln2_g=row(ln2_g[l]), ln2_b=row(ln2_b[l]),
    )
    return (_trunk(x_prompt, mem_prompt, p), _trunk(x_sample, mem_sample, p))
```

```python
import functools
import math

import numpy as np
import jax
import jax.numpy as jnp
from jax import lax
from jax.experimental import pallas as pl
from jax.experimental.pallas import tpu as pltpu

F32 = jnp.float32
BF16 = jnp.bfloat16

DIFF_HEADS = 8
DIFF_DK = 64
DIFF_DV = 128
DIL_CONFIGS = ((128, 1), (512, 4), (2048, 16))
DIL_HEADS_PER_GROUP = 4
DIL_HD = 64
MEM_HEADS = 4
MEM_HD = 64
TOP_K = 8
ROUTED_SCALE = 2.5
LN_EPS = 1e-5
NEG_INF = -1e30
DEPTH = 1
DEEPNORM_ALPHA = (2 * DEPTH) ** 0.25
LOG2E = math.log2(math.e)

D_MODEL = 1024
DIFF_QK = DIFF_HEADS * 2 * DIFF_DK
DIFF_V = DIFF_HEADS * DIFF_DV
N_DIL_HEADS = len(DIL_CONFIGS) * DIL_HEADS_PER_GROUP
DIL_W = N_DIL_HEADS * DIL_HD
DIL_OUT = DIL_HEADS_PER_GROUP * DIL_HD
MEM_Q = MEM_HEADS * MEM_HD
D_IN = 2 * DIFF_QK + DIFF_V + 3 * DIL_W + MEM_Q
SRC_VD = 2 * DIFF_QK
SRC_QL = SRC_VD + DIFF_V
SRC_KL = SRC_QL + DIL_W
SRC_VL = SRC_KL + DIL_W
SRC_QM = SRC_VL + DIL_W
D_MAIN = 2 * DIFF_QK + MEM_Q
OFF_QD, OFF_KD, OFF_QM = 0, DIFF_QK, 2 * DIFF_QK
D_DIL = 3 * DIL_OUT
SEQ_TILE = 512
POS_SPLIT = 64
ONES_ROWS = 16
SKIP_LOG2 = 160.0
NORM_MARGIN = 1.02

VMEM_LIMIT = 56 * 1024 * 1024
EXPERT_ROWS = 128
ROW_GROUP = 8
MOE_TILE = 4096


def _cparams(sem):
    return pltpu.CompilerParams(dimension_semantics=sem, vmem_limit_bytes=VMEM_LIMIT)


def _ln(x, g, b):
    mu = jnp.mean(x, axis=-1, keepdims=True)
    xc = x - mu
    var = jnp.mean(xc * xc, axis=-1, keepdims=True)
    return xc * lax.rsqrt(var + LN_EPS) * g + b


def _alibi_slopes(n):
    return np.asarray(2.0 ** (-8.0 * np.arange(1, n + 1) / n), dtype=np.float32)


def _ln_proj_kernel(x_ref, g_ref, b_ref, w_ref, wvt_ref, grp_ref, main_ref, d0_ref, d1_ref, d2_ref, vt_ref, nrm_ref,
                    xn_ref):
    xn_ref[...] = _ln(x_ref[...], g_ref[...], b_ref[...]).astype(BF16)
    col = 0
    for o_ref in (main_ref, d0_ref, d1_ref, d2_ref):
        for n in range(o_ref.shape[1] // D_DIL):
            o_ref[:, n * D_DIL:(n + 1) * D_DIL] = jnp.dot(
                xn_ref[...], w_ref[:, col:col + D_DIL], preferred_element_type=F32).astype(o_ref.dtype)
            col += D_DIL
    vt_ref[0] = lax.dot_general(wvt_ref[...], xn_ref[...], (((1,), (1,)), ((), ())),
                                preferred_element_type=F32).astype(vt_ref.dtype)
    qk = main_ref[:, :2 * DIFF_QK].astype(F32)
    sq = jnp.dot((qk * qk).astype(BF16), grp_ref[...], preferred_element_type=F32)
    nrm_ref[0] = jnp.max(sq, axis=0, keepdims=True)


def _ln_proj(x, g, b, w, wvt, tm=SEQ_TILE):
    T, D = x.shape
    N = w.shape[1]
    NV = wvt.shape[0]
    assert N == D_MAIN + 3 * D_DIL and D_MAIN % D_DIL == 0
    tok = lambda width: pl.BlockSpec((tm, width), lambda i: (i, 0))
    cols = lax.broadcasted_iota(jnp.int32, (2 * DIFF_QK, 128), 0)
    grp = (cols // DIFF_DK == lax.broadcasted_iota(jnp.int32, (2 * DIFF_QK, 128), 1)).astype(BF16)
    return pl.pallas_call(
        _ln_proj_kernel,
        name="ln_proj",
        grid=(T // tm,),
        in_specs=[pl.BlockSpec((tm, D), lambda i: (i, 0)),
                  pl.BlockSpec((1, D), lambda i: (0, 0)),
                  pl.BlockSpec((1, D), lambda i: (0, 0)),
                  pl.BlockSpec((D, N), lambda i: (0, 0)),
                  pl.BlockSpec((NV, D), lambda i: (0, 0)),
                  pl.BlockSpec((2 * DIFF_QK, 128), lambda i: (0, 0))],
        out_specs=[tok(D_MAIN), tok(D_DIL), tok(D_DIL), tok(D_DIL),
                   pl.BlockSpec((1, NV, tm), lambda i: (i, 0, 0)),
                   pl.BlockSpec((1, 1, 128), lambda i: (i, 0, 0))],
        out_shape=[jax.ShapeDtypeStruct((T, D_MAIN), BF16)] + [jax.ShapeDtypeStruct((T, D_DIL), BF16)] * 3
        + [jax.ShapeDtypeStruct((T // tm, NV, tm), BF16), jax.ShapeDtypeStruct((T // tm, 1, 128), F32)],
        scratch_shapes=[pltpu.VMEM((tm, D), BF16)],
        compiler_params=_cparams(("parallel",)),
    )(x, g, b, w, wvt, grp)


def _split3_bf16(x):
    hi = x.astype(BF16).astype(F32)
    mid = (x - hi).astype(BF16).astype(F32)
    return hi, mid, x - hi - mid


def _diff_attn_kernel(lo_ref, hi_ref, slope_ref, lam_ref, q_ref, k_ref, kpos_ref, vt_ref, g_ref, o_ref,
                      qx_ref, m_ref, acc_ref, sa_ref, sb_ref, *, tq, tk, lambda_init):
    h = pl.program_id(1)
    qi = pl.program_id(2)
    slope = slope_ref[h]
    q = q_ref[0]
    lane = lax.broadcasted_iota(jnp.int32, q.shape, 1)
    ipos = (qi * tq + lax.broadcasted_iota(jnp.int32, q.shape, 0)).astype(F32)
    a_hi, a_mid, a_lo = _split3_bf16(-slope * ipos)
    c = jnp.full(q.shape, slope, F32)
    c_hi = c.astype(BF16).astype(F32)
    c_lo = c - c_hi
    feats = (a_hi, a_mid, a_lo, POS_SPLIT * c_hi, c_hi, POS_SPLIT * c_lo, c_lo)
    ext = jnp.zeros(q.shape, F32)
    for n, f in enumerate(feats):
        ext = jnp.where(lane == n, f, ext)
    for comp in range(2):
        qz = jnp.where((lane >= comp * DIFF_DK) & (lane < (comp + 1) * DIFF_DK), q, jnp.zeros_like(q))
        qx_ref[comp, 0] = jnp.concatenate([qz, ext.astype(BF16)], axis=1)
        qx_ref[comp, 1] = jnp.concatenate([qz, (-ext).astype(BF16)], axis=1)
        qx_ref[comp, 2] = jnp.concatenate([qz, jnp.zeros_like(qz)], axis=1)
    m_ref[...] = jnp.full(m_ref.shape, NEG_INF, F32)
    acc_ref[...] = jnp.zeros(acc_ref.shape, F32)
    ones = jnp.where(lax.broadcasted_iota(jnp.int32, (ONES_ROWS, tk), 0) == 0, 1.0, 0.0).astype(BF16)

    def scores(j, side, comp):
        kx = jnp.concatenate([k_ref[0, j], kpos_ref[j]], axis=1)
        return lax.dot_general(kx, qx_ref[comp, side], (((1,), (1,)), ((), ())), preferred_element_type=F32)

    def softmax_pv(st, j, comp):
        vx = jnp.concatenate([vt_ref[0, j], ones], axis=0)
        m_old = m_ref[comp]
        m_new = jnp.maximum(m_old, jnp.max(st, axis=0, keepdims=True))
        pt = jnp.exp2(st - m_new).astype(BF16)
        acc_ref[comp] = jnp.exp2(m_old - m_new) * acc_ref[comp] + jnp.dot(vx, pt, preferred_element_type=F32)
        m_ref[comp] = m_new

    step = (pl.program_id(0) * pl.num_programs(1) + h) * pl.num_programs(2) + qi
    lo = lo_ref[step]
    n_pos = hi_ref[step] - lo + 1

    def key_block(b):
        j = lo + b - 1
        side = (j >= qi).astype(jnp.int32)
        return jnp.where(b == 0, qi, j + side), side

    def fill(b, dst_ref):
        j, side = key_block(b)
        for comp in range(2):
            dst_ref[comp] = scores(j, side, comp)

    def drain(b, src_ref):
        j, _ = key_block(b)
        for comp in range(2):
            softmax_pv(src_ref[comp], j, comp)

    def trip(t, carry):
        fill(2 * t + 1, sb_ref)
        drain(2 * t, sa_ref)
        fill(2 * t + 2, sa_ref)
        drain(2 * t + 1, sb_ref)
        return carry

    jpos = qi * tk + lax.broadcasted_iota(jnp.int32, (tk, tq), 0)
    ipos_t = qi * tq + lax.broadcasted_iota(jnp.int32, (tk, tq), 1)
    bias = slope * jnp.abs(jpos - ipos_t).astype(F32)
    for comp in range(2):
        sa_ref[comp] = scores(qi, 2, comp) - bias
    lax.fori_loop(0, (n_pos - 1) // 2, trip, 0)

    @pl.when(n_pos % 2 == 0)
    def _():
        fill(n_pos - 1, sb_ref)
        drain(n_pos - 2, sa_ref)
        drain(n_pos - 1, sb_ref)

    @pl.when(n_pos % 2 == 1)
    def _():
        drain(n_pos - 1, sa_ref)

    o0 = acc_ref[0, :DIFF_DV] / acc_ref[0, DIFF_DV:DIFF_DV + 1]
    o1 = acc_ref[1, :DIFF_DV] / acc_ref[1, DIFF_DV:DIFF_DV + 1]
    o = o0 - lam_ref[0] * o1
    ms = jnp.mean(o * o, axis=0, keepdims=True)
    o = o * lax.rsqrt(ms + LN_EPS) * g_ref[...] * (1.0 - lambda_init)
    o_ref[0] = o.T.astype(o_ref.dtype)


def _active_key_range(norms, slopes, tile):
    B, nb, _ = norms.shape
    nh = DIFF_HEADS * 2
    qn = (jnp.sqrt(norms[:, :, :nh]) * NORM_MARGIN).reshape(B, nb, DIFF_HEADS, 2)
    kn = (jnp.sqrt(norms[:, :, nh:2 * nh]) * NORM_MARGIN).reshape(B, nb, DIFF_HEADS, 2)
    blk = jnp.arange(nb, dtype=jnp.int32)
    gap = jnp.abs(blk[:, None] - blk[None, :])
    dmin = jnp.maximum((gap - 1) * tile + 1, 0).astype(F32)
    bound = (qn[:, :, None] * (kn[:, None, :] + kn[:, :, None])
             - (slopes[None, None, None, :, None] * dmin[None, :, :, None, None]))
    active = jnp.any(bound > -SKIP_LOG2, axis=-1) | (gap == 0)[None, :, :, None]
    lo = jnp.min(jnp.where(active, blk[None, None, :, None], nb), axis=2)
    hi = jnp.max(jnp.where(active, blk[None, None, :, None], -1), axis=2)
    to_bhi = lambda a: a.transpose(0, 2, 1).reshape(-1).astype(jnp.int32)
    return to_bhi(lo), to_bhi(hi)


def _diff_attn(proj, vt, norms, slopes, lam, g_col, lambda_init):
    B, S, _ = proj.shape
    tq = tk = min(SEQ_TILE, S)
    nk = S // tk
    pos = lax.broadcasted_iota(jnp.int32, (nk, tk, 128), 0) * tk + lax.broadcasted_iota(jnp.int32, (nk, tk, 128), 1)
    lane = lax.broadcasted_iota(jnp.int32, (nk, tk, 128), 2)
    kpos = jnp.where(lane < 3, 1, jnp.where((lane == 3) | (lane == 5), pos // POS_SPLIT,
                                            jnp.where((lane == 4) | (lane == 6), pos % POS_SPLIT, 0))).astype(BF16)
    lo, hi = _active_key_range(norms.reshape(B, nk, 128), slopes, tk)
    qb, kb = OFF_QD // 128, OFF_KD // 128
    grid_spec = pltpu.PrefetchScalarGridSpec(
        num_scalar_prefetch=2,
        grid=(B, DIFF_HEADS, S // tq),
        in_specs=[pl.BlockSpec(memory_space=pltpu.SMEM),
                  pl.BlockSpec(memory_space=pltpu.SMEM),
                  pl.BlockSpec((1, tq, 128), lambda b, h, i, lo, hi: (b, i, qb + h)),
                  pl.BlockSpec((1, nk, tk, 128), lambda b, h, i, lo, hi: (b, 0, 0, kb + h)),
                  pl.BlockSpec((nk, tk, 128), lambda b, h, i, lo, hi: (0, 0, 0)),
                  pl.BlockSpec((1, nk, DIFF_DV, tk), lambda b, h, i, lo, hi: (b, 0, h, 0)),
                  pl.BlockSpec((DIFF_DV, 1), lambda b, h, i, lo, hi: (0, 0))],
        out_specs=pl.BlockSpec((1, tq, 128), lambda b, h, i, lo, hi: (b, i, h)),
        scratch_shapes=[pltpu.VMEM((2, 3, tq, 256), BF16), pltpu.VMEM((2, 1, tq), F32),
                        pltpu.VMEM((2, DIFF_DV + ONES_ROWS, tq), F32),
                        pltpu.VMEM((2, tk, tq), F32), pltpu.VMEM((2, tk, tq), F32)],
    )
    return pl.pallas_call(
        functools.partial(_diff_attn_kernel, tq=tq, tk=tk, lambda_init=lambda_init),
        name="diff_attn",
        grid_spec=grid_spec,
        out_shape=jax.ShapeDtypeStruct((B, S, DIFF_V), BF16),
        compiler_params=_cparams(("parallel", "parallel", "parallel")),
    )(lo, hi, slopes, lam, proj, proj.reshape(B, nk, tk, -1), kpos, vt.reshape(B, nk, DIFF_V, tk), g_col)


def _dilated_kernel(slope_ref, q_ref, kp_ref, kc_ref, kn_ref, vp_ref, vc_ref, vn_ref, o_ref, lse_ref,
                    *, tl, half, length, group, dilation):
    i = pl.program_id(2)
    q = q_ref[0]
    kcat = jnp.concatenate([kp_ref[0], kc_ref[0], kn_ref[0]], axis=0)
    vcat = jnp.concatenate([vp_ref[0], vc_ref[0], vn_ref[0]], axis=0)
    lane = lax.broadcasted_iota(jnp.int32, q.shape, 1)
    qpos = i * tl + lax.broadcasted_iota(jnp.int32, (tl, 3 * tl), 0)
    kpos = (i - 1) * tl + lax.broadcasted_iota(jnp.int32, (tl, 3 * tl), 1)
    rel = jnp.abs(qpos - kpos)
    ok = (rel <= half) & (kpos >= 0) & (kpos < length)
    relf = rel.astype(F32)
    olane = lax.broadcasted_iota(jnp.int32, (tl, DIL_OUT), 1)
    o_acc = jnp.zeros((tl, DIL_OUT), F32)
    lse_acc = jnp.zeros((tl, DIL_OUT), F32)
    for h in range(DIL_HEADS_PER_GROUP):
        sel = (lane >= h * DIL_HD) & (lane < (h + 1) * DIL_HD)
        qz = jnp.where(sel, q, jnp.zeros_like(q))
        s = lax.dot_general(qz, kcat, (((1,), (1,)), ((), ())), preferred_element_type=F32)
        s = s - (slope_ref[group * DIL_HEADS_PER_GROUP + h] * dilation) * relf
        s = jnp.where(ok, s, NEG_INF)
        m = jnp.max(s, axis=-1, keepdims=True)
        p = jnp.exp(s - m)
        l = jnp.sum(p, axis=-1, keepdims=True)
        pv = jnp.dot(p.astype(BF16), vcat, preferred_element_type=F32) / l
        osel = (olane >= h * DIL_HD) & (olane < (h + 1) * DIL_HD)
        o_acc = jnp.where(osel, pv, o_acc)
        lse_acc = jnp.where(osel, m + jnp.log(l), lse_acc)
    o_ref[0] = o_acc.astype(o_ref.dtype)
    lse_ref[0] = lse_acc


def _dilated_group(qkv, slopes, group, tl=128):
    B, S, _ = qkv.shape
    window, dil = DIL_CONFIGS[group]
    L = S // dil
    half = window // (2 * dil)
    tl = min(tl, L)
    assert half <= tl and L % tl == 0
    nb = L // tl
    pv = qkv.reshape(B, L, dil * D_DIL)
    nc = D_DIL // DIL_OUT
    qc, kc, vc = 0, 1, 2

    def spec(col, shift):
        def imap(b, r, i):
            return (b, jnp.clip(i + shift, 0, nb - 1), r * nc + col)
        return pl.BlockSpec((1, tl, DIL_OUT), imap)

    out_spec = pl.BlockSpec((1, tl, DIL_OUT), lambda b, r, i: (b, i, r))
    o, lse = pl.pallas_call(
        functools.partial(_dilated_kernel, tl=tl, half=half, length=L, group=group, dilation=float(dil)),
        name=f"dilated_attn_{group}",
        grid=(B, dil, nb),
        in_specs=[pl.BlockSpec(memory_space=pltpu.SMEM),
                  spec(qc, 0), spec(kc, -1), spec(kc, 0), spec(kc, 1),
                  spec(vc, -1), spec(vc, 0), spec(vc, 1)],
        out_specs=[out_spec, out_spec],
        out_shape=[jax.ShapeDtypeStruct((B, L, dil * DIL_OUT), BF16),
                   jax.ShapeDtypeStruct((B, L, dil * DIL_OUT), F32)],
        compiler_params=_cparams(("parallel", "parallel", "parallel")),
    )(slopes, pv, pv, pv, pv, pv, pv, pv)
    return o.reshape(B, S, DIL_OUT), lse.reshape(B, S, DIL_OUT)


def _mem_kv_kernel(m_ref, g_ref, b_ref, w_ref, o_ref):
    mn = _ln(m_ref[...], g_ref[...], b_ref[...]).astype(BF16)
    o_ref[...] = jnp.dot(mn, w_ref[...], preferred_element_type=F32).astype(o_ref.dtype)


def _mem_kv(mem2d, g, b, w, tm=256):
    T, D = mem2d.shape
    N = w.shape[1]
    return pl.pallas_call(
        _mem_kv_kernel,
        name="mem_kv",
        grid=(T // tm,),
        in_specs=[pl.BlockSpec((tm, D), lambda i: (i, 0)),
                  pl.BlockSpec((1, D), lambda i: (0, 0)),
                  pl.BlockSpec((1, D), lambda i: (0, 0)),
                  pl.BlockSpec((D, N), lambda i: (0, 0))],
        out_specs=pl.BlockSpec((tm, N), lambda i: (i, 0)),
        out_shape=jax.ShapeDtypeStruct((T, N), BF16),
        compiler_params=_cparams(("parallel",)),
    )(mem2d, g, b, w)


def _mem_attn_kernel(q_ref, kv_ref, o_ref):
    q = q_ref[0]
    k = kv_ref[0, :, :MEM_Q]
    v = kv_ref[0, :, MEM_Q:]
    lane = lax.broadcasted_iota(jnp.int32, q.shape, 1)
    o_acc = jnp.zeros(q.shape, F32)
    for h in range(MEM_HEADS):
        sel = (lane >= h * MEM_HD) & (lane < (h + 1) * MEM_HD)
        qz = jnp.where(sel, q, jnp.zeros_like(q))
        s = lax.dot_general(qz, k, (((1,), (1,)), ((), ())), preferred_element_type=F32)
        m = jnp.max(s, axis=-1, keepdims=True)
        p = jnp.exp(s - m)
        l = jnp.sum(p, axis=-1, keepdims=True)
        pv = jnp.dot(p.astype(BF16), v, preferred_element_type=F32) / l
        o_acc = jnp.where(sel, pv, o_acc)
    o_ref[0] = o_acc.astype(o_ref.dtype)


def _mem_attn(proj, kv, tm=512):
    B, S, _ = proj.shape
    M = kv.shape[1]
    tm = min(tm, S)
    qc = OFF_QM // MEM_Q
    return pl.pallas_call(
        _mem_attn_kernel,
        name="mem_attn",
        grid=(B, S // tm),
        in_specs=[pl.BlockSpec((1, tm, MEM_Q), lambda b, i: (b, i, qc)),
                  pl.BlockSpec((1, M, 2 * MEM_Q), lambda b, i: (b, 0, 0))],
        out_specs=pl.BlockSpec((1, tm, MEM_Q), lambda b, i: (b, i, 0)),
        out_shape=jax.ShapeDtypeStruct((B, S, MEM_Q), BF16),
        compiler_params=_cparams(("parallel", "parallel")),
    )(proj, kv)


HI16 = -65536


def _pack_halves(x):
    half = x.shape[1] // 2
    lo = lax.bitcast_convert_type(x[:, :half].astype(BF16).astype(F32), jnp.int32)
    hi = lax.bitcast_convert_type(x[:, half:].astype(BF16).astype(F32), jnp.int32)
    return lax.shift_right_logical(lo, 16) | (hi & HI16)


def _unpack_halves(r):
    lo = lax.bitcast_convert_type(lax.shift_left(r, 16), F32).astype(BF16)
    hi = lax.bitcast_convert_type(r & HI16, F32).astype(BF16)
    return lo, hi


def _merge_kernel(x_ref, od_ref, o0_ref, o1_ref, o2_ref, l0_ref, l1_ref, l2_ref, om_ref,
                  lng_ref, lnb_ref, wg_ref, bg_ref, wbd_ref, wbl_ref, wbm_ref, wo_ref, g1_ref, b1_ref,
                  x1_ref):
    D = x_ref.shape[1]
    xn = _ln(x_ref[...], lng_ref[...], lnb_ref[...])
    xb = xn.astype(BF16)
    l0, l1, l2 = l0_ref[...], l1_ref[...], l2_ref[...]
    mx = jnp.maximum(jnp.maximum(l0, l1), l2)
    e0, e1, e2 = jnp.exp(l0 - mx), jnp.exp(l1 - mx), jnp.exp(l2 - mx)
    ol = (e0 * o0_ref[...].astype(F32) + e1 * o1_ref[...].astype(F32) + e2 * o2_ref[...].astype(F32)) / (e0 + e1 + e2)
    branches = (jnp.dot(od_ref[...], wbd_ref[...], preferred_element_type=F32),
                jnp.dot(ol.astype(BF16), wbl_ref[...], preferred_element_type=F32),
                jnp.dot(om_ref[...], wbm_ref[...], preferred_element_type=F32))
    merged = jnp.zeros(xn.shape, F32)
    for n, br in enumerate(branches):
        cols = slice(n * D, (n + 1) * D)
        gate = jax.nn.sigmoid(jnp.dot(xb, wg_ref[:, cols], preferred_element_type=F32) + bg_ref[:, cols])
        merged = merged + gate * br
    y = DEEPNORM_ALPHA * xn + jnp.dot(merged.astype(BF16), wo_ref[...], preferred_element_type=F32)
    x1_ref[...] = _ln(y, g1_ref[...], b1_ref[...])


def _merge(x, od, odil, lses, om, lng, lnb, wg, bg, wbd, wbl, wbm, wo, g1, b1, tm=256):
    T, D = x.shape
    row = lambda w: pl.BlockSpec((tm, w), lambda i: (i, 0))
    full = lambda a: pl.BlockSpec(a.shape, lambda i: (0,) * a.ndim)
    params = (lng, lnb, wg, bg, wbd, wbl, wbm, wo, g1, b1)
    return pl.pallas_call(
        _merge_kernel,
        name="merge_out",
        grid=(T // tm,),
        in_specs=[row(D), row(D)] + [row(DIL_OUT)] * 7 + [full(a) for a in params],
        out_specs=row(D),
        out_shape=jax.ShapeDtypeStruct((T, D), F32),
        compiler_params=_cparams(("parallel",)),
    )(x, od, *odil, *lses, om, *params)


def _router_kernel(x_ref, w_ref, bias_ref, idx_ref, xr_ref):
    x = x_ref[...]
    logits = jnp.dot(x.astype(BF16), w_ref[...], preferred_element_type=F32)
    scores = jax.nn.sigmoid(logits)
    sel = scores + bias_ref[...]
    tm, E = sel.shape
    lane = lax.broadcasted_iota(jnp.int32, (tm, E), 1)
    olane = lax.broadcasted_iota(jnp.int32, idx_ref.shape, 1)
    idx_out = jnp.zeros(idx_ref.shape, jnp.int32)
    chosen = jnp.zeros((tm, E), jnp.bool_)
    for k in range(TOP_K):
        mx = jnp.max(sel, axis=-1, keepdims=True)
        idx = jnp.min(jnp.where(sel == mx, lane, E), axis=-1, keepdims=True)
        hit = lane == idx
        chosen = chosen | hit
        sel = jnp.where(hit, -jnp.inf, sel)
        idx_out = jnp.where(olane == k, idx, idx_out)
    picked = jnp.where(chosen, scores, 0.0)
    idx_ref[...] = idx_out
    rw = picked / jnp.sum(picked, axis=-1, keepdims=True) * ROUTED_SCALE
    half = x.shape[1] // 2
    xr_ref[:, :half] = _pack_halves(x)
    xr_ref[:, half:] = lax.bitcast_convert_type(rw, jnp.int32)


def _router(x1, w, bias, tm=512):
    T, D = x1.shape
    E = w.shape[1]
    return pl.pallas_call(
        _router_kernel,
        name="router",
        grid=(T // tm,),
        in_specs=[pl.BlockSpec((tm, D), lambda i: (i, 0)),
                  pl.BlockSpec((D, E), lambda i: (0, 0)),
                  pl.BlockSpec((1, E), lambda i: (0, 0))],
        out_specs=[pl.BlockSpec((tm, 128), lambda i: (i, 0)), pl.BlockSpec((tm, D // 2 + E), lambda i: (i, 0))],
        out_shape=[jax.ShapeDtypeStruct((T, 128), jnp.int32), jax.ShapeDtypeStruct((T, D // 2 + E), jnp.int32)],
        compiler_params=_cparams(("parallel",)),
    )(x1, w, bias)


def _moe_kernel(offs_ref, tok_ref, xr_ref, wg_ref, wu_ref, wd_ref, out_ref, rows_ref, y_ref, *, chunk, n_experts):
    tile = pl.program_id(0)
    e = pl.program_id(1)

    @pl.when(e == 0)
    def _():
        out_ref[...] = jnp.zeros(out_ref.shape, F32)

    @pl.when((tile == 0) & (e == 0))
    def _():
        rows_ref[...] = jnp.zeros(rows_ref.shape, rows_ref.dtype)

    start = offs_ref[tile * (n_experts + 1) + e]
    n = offs_ref[tile * (n_experts + 1) + e + 1] - start
    half = wg_ref.shape[1] // 2

    def one_chunk(c, carry):
        base = start + c * chunk
        cnt = jnp.minimum(chunk, n - c * chunk)

        for g in range(chunk // ROW_GROUP):
            @pl.when(g * ROW_GROUP < cnt)
            def _():
                for s in range(ROW_GROUP):
                    i = g * ROW_GROUP + s
                    t = tok_ref[0, 0, base + i]
                    rows_ref[i:i + 1, :] = xr_ref[pl.ds(t, 1), :]

        lo, hi = _unpack_halves(rows_ref[:, :half])
        g = (jnp.dot(lo, wg_ref[0, :half], preferred_element_type=F32)
             + jnp.dot(hi, wg_ref[0, half:], preferred_element_type=F32))
        u = (jnp.dot(lo, wu_ref[0, :half], preferred_element_type=F32)
             + jnp.dot(hi, wu_ref[0, half:], preferred_element_type=F32))
        wrow = lax.bitcast_convert_type(rows_ref[:, half:], F32)
        lane = lax.broadcasted_iota(jnp.int32, wrow.shape, 1)
        wcol = jnp.sum(jnp.where(lane == e, wrow, 0.0), axis=1, keepdims=True)
        wcol = jnp.where(lax.broadcasted_iota(jnp.int32, wcol.shape, 0) < cnt, wcol, 0.0)
        hid = (g * jax.nn.sigmoid(g)) * u * wcol
        y_ref[...] = jnp.dot(hid.astype(BF16), wd_ref[0], preferred_element_type=F32)

        for g in range(chunk // ROW_GROUP):
            @pl.when((g + 1) * ROW_GROUP <= cnt)
            def _():
                toks = [tok_ref[0, 0, base + g * ROW_GROUP + s] for s in range(ROW_GROUP)]
                olds = [out_ref[pl.ds(t, 1), :] for t in toks]
                for s in range(ROW_GROUP):
                    i = g * ROW_GROUP + s
                    out_ref[pl.ds(toks[s], 1), :] = olds[s] + y_ref[i:i + 1, :]

        def scatter_row(i, carry):
            t = tok_ref[0, 0, base + i]
            out_ref[pl.ds(t, 1), :] += y_ref[pl.ds(i, 1), :]
            return carry

        lax.fori_loop(cnt // ROW_GROUP * ROW_GROUP, cnt, scatter_row, 0)
        return carry

    lax.fori_loop(0, (n + chunk - 1) // chunk, one_chunk, 0)


def _moe_experts(offs, tok, xr, wg, wu, wd, tile):
    T, W = xr.shape
    E, D, F = wg.shape
    assert W == D // 2 + E
    nt = T // tile
    grid_spec = pltpu.PrefetchScalarGridSpec(
        num_scalar_prefetch=1,
        grid=(nt, E),
        in_specs=[pl.BlockSpec((1, 1, tok.shape[2]), lambda t, e, offs: (t, 0, 0), memory_space=pltpu.SMEM),
                  pl.BlockSpec((tile, W), lambda t, e, offs: (t, 0), pipeline_mode=pl.Buffered(1)),
                  pl.BlockSpec((1, D, F), lambda t, e, offs: (e, 0, 0)),
                  pl.BlockSpec((1, D, F), lambda t, e, offs: (e, 0, 0)),
                  pl.BlockSpec((1, F, D), lambda t, e, offs: (e, 0, 0))],
        out_specs=pl.BlockSpec((tile, D), lambda t, e, offs: (t, 0)),
        scratch_shapes=[pltpu.VMEM((EXPERT_ROWS, W), jnp.int32), pltpu.VMEM((EXPERT_ROWS, D), F32)],
    )
    return pl.pallas_call(
        functools.partial(_moe_kernel, chunk=EXPERT_ROWS, n_experts=E),
        name="moe_experts",
        grid_spec=grid_spec,
        out_shape=jax.ShapeDtypeStruct((T, D), F32),
        compiler_params=_cparams(("arbitrary", "arbitrary")),
    )(offs, tok, xr, wg, wu, wd)


def _final_kernel(x1_ref, r_ref, wg_ref, wu_ref, wd_ref, g_ref, b_ref, o_ref):
    xb = x1_ref[...].astype(BF16)
    g = jnp.dot(xb, wg_ref[...], preferred_element_type=F32)
    u = jnp.dot(xb, wu_ref[...], preferred_element_type=F32)
    hid = (g * jax.nn.sigmoid(g)) * u
    shared = jnp.dot(hid.astype(BF16), wd_ref[...], preferred_element_type=F32)
    y = DEEPNORM_ALPHA * x1_ref[...] + (r_ref[...] + shared)
    o_ref[...] = _ln(y, g_ref[...], b_ref[...])


def _final(x1, routed, wg, wu, wd, g, b, tm=512):
    T, D = x1.shape
    row = pl.BlockSpec((tm, D), lambda i: (i, 0))
    full = lambda a: pl.BlockSpec(a.shape, lambda i: (0,) * a.ndim)
    params = (wg, wu, wd, g, b)
    return pl.pallas_call(
        _final_kernel,
        name="shared_final",
        grid=(T // tm,),
        in_specs=[row, row] + [full(a) for a in params],
        out_specs=row,
        out_shape=jax.ShapeDtypeStruct((T, D), F32),
        compiler_params=_cparams(("parallel",)),
    )(x1, routed, *params)


def _moe_routed(xr, top_idx, wg, wu, wd):
    T = xr.shape[0]
    E = wg.shape[0]
    tile = min(MOE_TILE, T)
    nt = T // tile
    e_tile = top_idx.reshape(nt, tile * TOP_K)
    order = jnp.argsort(e_tile, axis=1)
    e_sorted = jnp.take_along_axis(e_tile, order, axis=1)
    tok = jnp.pad((order // TOP_K).astype(jnp.int32), ((0, 0), (0, ROW_GROUP))).reshape(nt, 1, -1)
    bounds = jnp.arange(E + 1, dtype=jnp.int32)
    offs = jax.vmap(lambda row: jnp.searchsorted(row, bounds, side='left'))(e_sorted).astype(jnp.int32)
    return _moe_experts(offs.reshape(-1), tok, xr, wg, wu, wd, tile)


def _trunk(x, mem, p):
    B, S, D = x.shape
    T = B * S
    x2 = x.reshape(T, D)
    proj, d0, d1, d2, vt, norms = _ln_proj(x2, p['ln_in_g'], p['ln_in_b'], p['w_in'], p['w_vt'])
    proj = proj.reshape(B, S, D_MAIN)
    od = _diff_attn(proj, vt, norms, p['diff_slopes'], p['lam'], p['diff_subln_g'], p['lambda_init'])
    dil = [_dilated_group(d.reshape(B, S, D_DIL), p['dil_slopes'], g) for g, d in enumerate((d0, d1, d2))]
    kv = _mem_kv(mem.reshape(-1, D), p['ln_mem_g'], p['ln_mem_b'], p['w_mem_kv']).reshape(B, mem.shape[1], 2 * MEM_Q)
    om = _mem_attn(proj, kv)
    x1 = _merge(x2, od.reshape(T, DIFF_V), [o.reshape(T, DIL_OUT) for o, _ in dil],
                [l.reshape(T, DIL_OUT) for _, l in dil], om.reshape(T, MEM_Q),
                p['ln_in_g'], p['ln_in_b'], p['w_gate'], p['b_gate'], p['w_br_diff'], p['w_br_dil'],
                p['w_br_mem'], p['w_out'], p['ln1_g'], p['ln1_b'])
    idx, xr = _router(x1, p['w_router'], p['router_bias'])
    routed = _moe_routed(xr, idx[:, :TOP_K], p['w_e_gate'], p['w_e_up'], p['w_e_down'])
    y = _final(x1, routed, p['w_s_gate'], p['w_s_up'], p['w_s_down'], p['ln2_g'], p['ln2_b'])
    return y.reshape(B, S, D)


def kernel(x_prompt, x_sample, mem_prompt, mem_sample, ln_in_g, ln_in_b, ln_mem_g, ln_mem_b, w_in, w_gate, b_gate,
           lambda_q1, lambda_k1, lambda_q2, lambda_k2, diff_subln_g, w_br_diff, w_br_dil, w_br_mem, w_mem_kv,
           w_out, ln1_g, ln1_b, w_router, router_bias, w_e_gate, w_e_up, w_e_down, w_s_gate, w_s_up, w_s_down,
           ln2_g, ln2_b):
    l = 0
    lambda_init = 0.8 - 0.6 * math.exp(-0.3 * l)
    lam = (jnp.exp(jnp.sum(lambda_q1[l] * lambda_k1[l])) - jnp.exp(jnp.sum(lambda_q2[l] * lambda_k2[l]))
           + lambda_init).astype(F32).reshape(1)
    colscale = np.ones((D_IN,), np.float32)
    for off, width, hd in ((0, DIFF_QK, DIFF_DK), (SRC_QL, DIL_W, DIL_HD), (SRC_QM, MEM_Q, MEM_HD)):
        colscale[off:off + width] = hd ** -0.5
    colscale[:DIFF_QK] *= LOG2E
    w_all = (w_in[l] * colscale).astype(BF16)
    cols = lambda off, width: w_all[:, off:off + width]
    w_groups = [jnp.concatenate([cols(src + g * DIL_OUT, DIL_OUT) for src in (SRC_QL, SRC_KL, SRC_VL)], axis=1)
                for g in range(len(DIL_CONFIGS))]
    row = lambda a: a.reshape(1, -1).astype(F32)
    p = dict(
        ln_in_g=row(ln_in_g), ln_in_b=row(ln_in_b), ln_mem_g=row(ln_mem_g[l]), ln_mem_b=row(ln_mem_b[l]),
        w_in=jnp.concatenate([cols(0, SRC_VD), cols(SRC_QM, MEM_Q)] + w_groups, axis=1),
        w_vt=cols(SRC_VD, DIFF_V).T,
        w_gate=w_gate[l].astype(BF16), b_gate=row(b_gate[l]),
        lam=lam, lambda_init=lambda_init, diff_subln_g=diff_subln_g[l].reshape(-1, 1).astype(F32),
        diff_slopes=jnp.asarray(_alibi_slopes(DIFF_HEADS) * np.float32(LOG2E)), dil_slopes=jnp.asarray(_alibi_slopes(N_DIL_HEADS)),
        w_br_diff=w_br_diff[l].astype(BF16), w_br_dil=w_br_dil[l].astype(BF16), w_br_mem=w_br_mem[l].astype(BF16),
        w_mem_kv=w_mem_kv[l].astype(BF16), w_out=w_out[l].astype(BF16), ln1_g=row(ln1_g[l]), ln1_b=row(ln1_b[l]),
        w_router=w_router[l].astype(BF16), router_bias=row(router_bias[l]),
        w_e_gate=w_e_gate[l].astype(BF16), w_e_up=w_e_up[l].astype(BF16), w_e_down=w_e_down[l].astype(BF16),
        w_s_gate=w_s_gate[l].astype(BF16), w_s_up=w_s_up[l].astype(BF16), w_s_down=w_s_down[l].astype(BF16),
        ln2_g=row(ln2_g[l]), ln2_b=row(ln2_b[l]),
    )
    return (_trunk(x_prompt, mem_prompt, p), _trunk(x_sample, mem_sample, p))
```

```python
import functools
import math

import numpy as np
import jax
import jax.numpy as jnp
from jax import lax
from jax.experimental import pallas as pl
from jax.experimental.pallas import tpu as pltpu

F32 = jnp.float32
BF16 = jnp.bfloat16

DIFF_HEADS = 8
DIFF_DK = 64
DIFF_DV = 128
DIL_CONFIGS = ((128, 1), (512, 4), (2048, 16))
DIL_HEADS_PER_GROUP = 4
DIL_HD = 64
MEM_HEADS = 4
MEM_HD = 64
TOP_K = 8
ROUTED_SCALE = 2.5
LN_EPS = 1e-5
NEG_INF = -1e30
DEPTH = 1
DEEPNORM_ALPHA = (2 * DEPTH) ** 0.25
LOG2E = math.log2(math.e)

D_MODEL = 1024
DIFF_QK = DIFF_HEADS * 2 * DIFF_DK
DIFF_V = DIFF_HEADS * DIFF_DV
N_DIL_HEADS = len(DIL_CONFIGS) * DIL_HEADS_PER_GROUP
DIL_W = N_DIL_HEADS * DIL_HD
DIL_OUT = DIL_HEADS_PER_GROUP * DIL_HD
MEM_Q = MEM_HEADS * MEM_HD
D_IN = 2 * DIFF_QK + DIFF_V + 3 * DIL_W + MEM_Q
SRC_VD = 2 * DIFF_QK
SRC_QL = SRC_VD + DIFF_V
SRC_KL = SRC_QL + DIL_W
SRC_VL = SRC_KL + DIL_W
SRC_QM = SRC_VL + DIL_W
D_MAIN = 2 * DIFF_QK + MEM_Q
OFF_QD, OFF_KD, OFF_QM = 0, DIFF_QK, 2 * DIFF_QK
D_DIL = 3 * DIL_OUT
SEQ_TILE = 512
POS_SPLIT = 64
ONES_ROWS = 16
SKIP_LOG2 = 160.0
NORM_MARGIN = 1.02

VMEM_LIMIT = 56 * 1024 * 1024
EXPERT_ROWS = 160
EXPERTS_PER_STEP = 4
ROW_GROUP = 8
MOE_TILE = 4096


def _cparams(sem):
    return pltpu.CompilerParams(dimension_semantics=sem, vmem_limit_bytes=VMEM_LIMIT)


def _ln(x, g, b):
    mu = jnp.mean(x, axis=-1, keepdims=True)
    xc = x - mu
    var = jnp.mean(xc * xc, axis=-1, keepdims=True)
    return xc * lax.rsqrt(var + LN_EPS) * g + b


def _alibi_slopes(n):
    return np.asarray(2.0 ** (-8.0 * np.arange(1, n + 1) / n), dtype=np.float32)


def _ln_proj_kernel(x_ref, g_ref, b_ref, w_ref, wvt_ref, grp_ref, main_ref, d0_ref, d1_ref, d2_ref, vt_ref, nrm_ref,
                    xn_ref):
    xn_ref[...] = _ln(x_ref[...], g_ref[...], b_ref[...]).astype(BF16)
    col = 0
    for o_ref in (main_ref, d0_ref, d1_ref, d2_ref):
        for n in range(o_ref.shape[1] // D_DIL):
            o_ref[:, n * D_DIL:(n + 1) * D_DIL] = jnp.dot(
                xn_ref[...], w_ref[:, col:col + D_DIL], preferred_element_type=F32).astype(o_ref.dtype)
            col += D_DIL
    vt_ref[0] = lax.dot_general(wvt_ref[...], xn_ref[...], (((1,), (1,)), ((), ())),
                                preferred_element_type=F32).astype(vt_ref.dtype)
    qk = main_ref[:, :2 * DIFF_QK].astype(F32)
    sq = jnp.dot((qk * qk).astype(BF16), grp_ref[...], preferred_element_type=F32)
    nrm_ref[0] = jnp.max(sq, axis=0, keepdims=True)


def _ln_proj(x, g, b, w, wvt, tm=SEQ_TILE):
    T, D = x.shape
    N = w.shape[1]
    NV = wvt.shape[0]
    assert N == D_MAIN + 3 * D_DIL and D_MAIN % D_DIL == 0
    tok = lambda width: pl.BlockSpec((tm, width), lambda i: (i, 0))
    cols = lax.broadcasted_iota(jnp.int32, (2 * DIFF_QK, 128), 0)
    grp = (cols // DIFF_DK == lax.broadcasted_iota(jnp.int32, (2 * DIFF_QK, 128), 1)).astype(BF16)
    return pl.pallas_call(
        _ln_proj_kernel,
        name="ln_proj",
        grid=(T // tm,),
        in_specs=[pl.BlockSpec((tm, D), lambda i: (i, 0)),
                  pl.BlockSpec((1, D), lambda i: (0, 0)),
                  pl.BlockSpec((1, D), lambda i: (0, 0)),
                  pl.BlockSpec((D, N), lambda i: (0, 0)),
                  pl.BlockSpec((NV, D), lambda i: (0, 0)),
                  pl.BlockSpec((2 * DIFF_QK, 128), lambda i: (0, 0))],
        out_specs=[tok(D_MAIN), tok(D_DIL), tok(D_DIL), tok(D_DIL),
                   pl.BlockSpec((1, NV, tm), lambda i: (i, 0, 0)),
                   pl.BlockSpec((1, 1, 128), lambda i: (i, 0, 0))],
        out_shape=[jax.ShapeDtypeStruct((T, D_MAIN), BF16)] + [jax.ShapeDtypeStruct((T, D_DIL), BF16)] * 3
        + [jax.ShapeDtypeStruct((T // tm, NV, tm), BF16), jax.ShapeDtypeStruct((T // tm, 1, 128), F32)],
        scratch_shapes=[pltpu.VMEM((tm, D), BF16)],
        compiler_params=_cparams(("parallel",)),
    )(x, g, b, w, wvt, grp)


def _split3_bf16(x):
    hi = x.astype(BF16).astype(F32)
    mid = (x - hi).astype(BF16).astype(F32)
    return hi, mid, x - hi - mid


def _diff_attn_kernel(lo_ref, hi_ref, slope_ref, lam_ref, q_ref, k_ref, kpos_ref, vt_ref, g_ref, o_ref,
                      qx_ref, m_ref, acc_ref, sa_ref, sb_ref, *, tq, tk, lambda_init):
    h = pl.program_id(1)
    qi = pl.program_id(2)
    slope = slope_ref[h]
    q = q_ref[0]
    lane = lax.broadcasted_iota(jnp.int32, q.shape, 1)
    ipos = (qi * tq + lax.broadcasted_iota(jnp.int32, q.shape, 0)).astype(F32)
    a_hi, a_mid, a_lo = _split3_bf16(-slope * ipos)
    c = jnp.full(q.shape, slope, F32)
    c_hi = c.astype(BF16).astype(F32)
    c_lo = c - c_hi
    feats = (a_hi, a_mid, a_lo, POS_SPLIT * c_hi, c_hi, POS_SPLIT * c_lo, c_lo)
    ext = jnp.zeros(q.shape, F32)
    for n, f in enumerate(feats):
        ext = jnp.where(lane == n, f, ext)
    for comp in range(2):
        qz = jnp.where((lane >= comp * DIFF_DK) & (lane < (comp + 1) * DIFF_DK), q, jnp.zeros_like(q))
        qx_ref[comp, 0] = jnp.concatenate([qz, ext.astype(BF16)], axis=1)
        qx_ref[comp, 1] = jnp.concatenate([qz, (-ext).astype(BF16)], axis=1)
        qx_ref[comp, 2] = jnp.concatenate([qz, jnp.zeros_like(qz)], axis=1)
    m_ref[...] = jnp.full(m_ref.shape, NEG_INF, F32)
    acc_ref[...] = jnp.zeros(acc_ref.shape, F32)
    ones = jnp.where(lax.broadcasted_iota(jnp.int32, (ONES_ROWS, tk), 0) == 0, 1.0, 0.0).astype(BF16)

    def scores(j, side, comp):
        kx = jnp.concatenate([k_ref[0, j], kpos_ref[j]], axis=1)
        return lax.dot_general(kx, qx_ref[comp, side], (((1,), (1,)), ((), ())), preferred_element_type=F32)

    def softmax_pv(st, j, comp):
        vx = jnp.concatenate([vt_ref[0, j], ones], axis=0)
        m_old = m_ref[comp]
        m_new = jnp.maximum(m_old, jnp.max(st, axis=0, keepdims=True))
        pt = jnp.exp2(st - m_new).astype(BF16)
        acc_ref[comp] = jnp.exp2(m_old - m_new) * acc_ref[comp] + jnp.dot(vx, pt, preferred_element_type=F32)
        m_ref[comp] = m_new

    step = (pl.program_id(0) * pl.num_programs(1) + h) * pl.num_programs(2) + qi
    lo = lo_ref[step]
    n_pos = hi_ref[step] - lo + 1

    def key_block(b):
        j = lo + b - 1
        side = (j >= qi).astype(jnp.int32)
        return jnp.where(b == 0, qi, j + side), side

    def fill(b, dst_ref):
        j, side = key_block(b)
        for comp in range(2):
            dst_ref[comp] = scores(j, side, comp)

    def drain(b, src_ref):
        j, _ = key_block(b)
        for comp in range(2):
            softmax_pv(src_ref[comp], j, comp)

    def trip(t, carry):
        fill(2 * t + 1, sb_ref)
        drain(2 * t, sa_ref)
        fill(2 * t + 2, sa_ref)
        drain(2 * t + 1, sb_ref)
        return carry

    jpos = qi * tk + lax.broadcasted_iota(jnp.int32, (tk, tq), 0)
    ipos_t = qi * tq + lax.broadcasted_iota(jnp.int32, (tk, tq), 1)
    bias = slope * jnp.abs(jpos - ipos_t).astype(F32)
    for comp in range(2):
        sa_ref[comp] = scores(qi, 2, comp) - bias
    lax.fori_loop(0, (n_pos - 1) // 2, trip, 0)

    @pl.when(n_pos % 2 == 0)
    def _():
        fill(n_pos - 1, sb_ref)
        drain(n_pos - 2, sa_ref)
        drain(n_pos - 1, sb_ref)

    @pl.when(n_pos % 2 == 1)
    def _():
        drain(n_pos - 1, sa_ref)

    o0 = acc_ref[0, :DIFF_DV] / acc_ref[0, DIFF_DV:DIFF_DV + 1]
    o1 = acc_ref[1, :DIFF_DV] / acc_ref[1, DIFF_DV:DIFF_DV + 1]
    o = o0 - lam_ref[0] * o1
    ms = jnp.mean(o * o, axis=0, keepdims=True)
    o = o * lax.rsqrt(ms + LN_EPS) * g_ref[...] * (1.0 - lambda_init)
    o_ref[0] = o.T.astype(o_ref.dtype)


def _active_key_range(norms, slopes, tile):
    B, nb, _ = norms.shape
    nh = DIFF_HEADS * 2
    qn = (jnp.sqrt(norms[:, :, :nh]) * NORM_MARGIN).reshape(B, nb, DIFF_HEADS, 2)
    kn = (jnp.sqrt(norms[:, :, nh:2 * nh]) * NORM_MARGIN).reshape(B, nb, DIFF_HEADS, 2)
    blk = jnp.arange(nb, dtype=jnp.int32)
    gap = jnp.abs(blk[:, None] - blk[None, :])
    dmin = jnp.maximum((gap - 1) * tile + 1, 0).astype(F32)
    bound = (qn[:, :, None] * (kn[:, None, :] + kn[:, :, None])
             - (slopes[None, None, None, :, None] * dmin[None, :, :, None, None]))
    active = jnp.any(bound > -SKIP_LOG2, axis=-1) | (gap == 0)[None, :, :, None]
    lo = jnp.min(jnp.where(active, blk[None, None, :, None], nb), axis=2)
    hi = jnp.max(jnp.where(active, blk[None, None, :, None], -1), axis=2)
    to_bhi = lambda a: a.transpose(0, 2, 1).reshape(-1).astype(jnp.int32)
    return to_bhi(lo), to_bhi(hi)


def _diff_attn(proj, vt, norms, slopes, lam, g_col, lambda_init):
    B, S, _ = proj.shape
    tq = tk = min(SEQ_TILE, S)
    nk = S // tk
    pos = lax.broadcasted_iota(jnp.int32, (nk, tk, 128), 0) * tk + lax.broadcasted_iota(jnp.int32, (nk, tk, 128), 1)
    lane = lax.broadcasted_iota(jnp.int32, (nk, tk, 128), 2)
    kpos = jnp.where(lane < 3, 1, jnp.where((lane == 3) | (lane == 5), pos // POS_SPLIT,
                                            jnp.where((lane == 4) | (lane == 6), pos % POS_SPLIT, 0))).astype(BF16)
    lo, hi = _active_key_range(norms.reshape(B, nk, 128), slopes, tk)
    qb, kb = OFF_QD // 128, OFF_KD // 128
    grid_spec = pltpu.PrefetchScalarGridSpec(
        num_scalar_prefetch=2,
        grid=(B, DIFF_HEADS, S // tq),
        in_specs=[pl.BlockSpec(memory_space=pltpu.SMEM),
                  pl.BlockSpec(memory_space=pltpu.SMEM),
                  pl.BlockSpec((1, tq, 128), lambda b, h, i, lo, hi: (b, i, qb + h)),
                  pl.BlockSpec((1, nk, tk, 128), lambda b, h, i, lo, hi: (b, 0, 0, kb + h)),
                  pl.BlockSpec((nk, tk, 128), lambda b, h, i, lo, hi: (0, 0, 0)),
                  pl.BlockSpec((1, nk, DIFF_DV, tk), lambda b, h, i, lo, hi: (b, 0, h, 0)),
                  pl.BlockSpec((DIFF_DV, 1), lambda b, h, i, lo, hi: (0, 0))],
        out_specs=pl.BlockSpec((1, tq, 128), lambda b, h, i, lo, hi: (b, i, h)),
        scratch_shapes=[pltpu.VMEM((2, 3, tq, 256), BF16), pltpu.VMEM((2, 1, tq), F32),
                        pltpu.VMEM((2, DIFF_DV + ONES_ROWS, tq), F32),
                        pltpu.VMEM((2, tk, tq), F32), pltpu.VMEM((2, tk, tq), F32)],
    )
    return pl.pallas_call(
        functools.partial(_diff_attn_kernel, tq=tq, tk=tk, lambda_init=lambda_init),
        name="diff_attn",
        grid_spec=grid_spec,
        out_shape=jax.ShapeDtypeStruct((B, S, DIFF_V), BF16),
        compiler_params=_cparams(("parallel", "parallel", "parallel")),
    )(lo, hi, slopes, lam, proj, proj.reshape(B, nk, tk, -1), kpos, vt.reshape(B, nk, DIFF_V, tk), g_col)


def _dilated_kernel(slope_ref, q_ref, *refs, tl, half, n_kb, length, group, dilation):
    k_refs, v_refs, (o_ref, lse_ref) = refs[:n_kb], refs[n_kb:2 * n_kb], refs[2 * n_kb:]
    i = pl.program_id(2)
    q = q_ref[0]
    kcat = jnp.concatenate([r[0] for r in k_refs], axis=0)
    vcat = jnp.concatenate([r[0] for r in v_refs], axis=0)
    nkeys = n_kb * half
    nh = DIL_HEADS_PER_GROUP
    lane = lax.broadcasted_iota(jnp.int32, q.shape, 1)
    qstack = jnp.concatenate(
        [jnp.where((lane >= h * DIL_HD) & (lane < (h + 1) * DIL_HD), q, jnp.zeros_like(q)) for h in range(nh)], axis=0)
    row = lax.broadcasted_iota(jnp.int32, (nh * tl, 1), 0)
    slope = jnp.zeros((nh * tl, 1), F32)
    for h in range(nh):
        slope = jnp.where(row // tl == h, slope_ref[group * nh + h] * dilation, slope)
    qpos = i * tl + lax.broadcasted_iota(jnp.int32, (nh * tl, nkeys), 0) % tl
    kpos = i * tl - half + lax.broadcasted_iota(jnp.int32, (nh * tl, nkeys), 1)
    rel = jnp.abs(qpos - kpos)
    ok = (rel <= half) & (kpos >= 0) & (kpos < length)
    s = lax.dot_general(qstack, kcat, (((1,), (1,)), ((), ())), preferred_element_type=F32)
    s = jnp.where(ok, s - slope * rel.astype(F32), NEG_INF)
    m = jnp.max(s, axis=-1, keepdims=True)
    p = jnp.exp(s - m)
    l = jnp.sum(p, axis=-1, keepdims=True)
    pv = jnp.dot(p.astype(BF16), vcat, preferred_element_type=F32) / l
    lse = m + jnp.log(l)
    olane = lax.broadcasted_iota(jnp.int32, (tl, DIL_OUT), 1)
    o_acc = jnp.zeros((tl, DIL_OUT), F32)
    lse_acc = jnp.zeros((tl, DIL_OUT), F32)
    for h in range(nh):
        osel = (olane >= h * DIL_HD) & (olane < (h + 1) * DIL_HD)
        o_acc = jnp.where(osel, pv[h * tl:(h + 1) * tl], o_acc)
        lse_acc = jnp.where(osel, lse[h * tl:(h + 1) * tl], lse_acc)
    o_ref[0] = o_acc.astype(o_ref.dtype)
    lse_ref[0] = lse_acc


def _dilated_group(qkv, slopes, group, tl=128):
    B, S, _ = qkv.shape
    window, dil = DIL_CONFIGS[group]
    L = S // dil
    half = window // (2 * dil)
    tl = min(tl, L)
    assert tl % half == 0 and L % tl == 0
    per = tl // half
    n_kb = per + 2
    last_kb = L // half - 1
    pv = qkv.reshape(B, L, dil * D_DIL)
    nc = D_DIL // DIL_OUT
    qc, kc, vc = 0, 1, 2

    def kv_spec(col, n):
        def imap(b, r, i):
            return (b, jnp.clip(i * per - 1 + n, 0, last_kb), r * nc + col)
        return pl.BlockSpec((1, half, DIL_OUT), imap)

    q_spec = pl.BlockSpec((1, tl, DIL_OUT), lambda b, r, i: (b, i, r * nc + qc))
    out_spec = pl.BlockSpec((1, tl, DIL_OUT), lambda b, r, i: (b, i, r))
    o, lse = pl.pallas_call(
        functools.partial(_dilated_kernel, tl=tl, half=half, n_kb=n_kb, length=L, group=group, dilation=float(dil)),
        name=f"dilated_attn_{group}",
        grid=(B, dil, L // tl),
        in_specs=[pl.BlockSpec(memory_space=pltpu.SMEM), q_spec]
        + [kv_spec(kc, n) for n in range(n_kb)] + [kv_spec(vc, n) for n in range(n_kb)],
        out_specs=[out_spec, out_spec],
        out_shape=[jax.ShapeDtypeStruct((B, L, dil * DIL_OUT), BF16),
                   jax.ShapeDtypeStruct((B, L, dil * DIL_OUT), F32)],
        compiler_params=_cparams(("parallel", "parallel", "parallel")),
    )(slopes, *([pv] * (1 + 2 * n_kb)))
    return o.reshape(B, S, DIL_OUT), lse.reshape(B, S, DIL_OUT)


def _mem_kv_kernel(m_ref, g_ref, b_ref, w_ref, o_ref):
    mn = _ln(m_ref[...], g_ref[...], b_ref[...]).astype(BF16)
    o_ref[...] = jnp.dot(mn, w_ref[...], preferred_element_type=F32).astype(o_ref.dtype)


def _mem_kv(mem2d, g, b, w, tm=256):
    T, D = mem2d.shape
    N = w.shape[1]
    return pl.pallas_call(
        _mem_kv_kernel,
        name="mem_kv",
        grid=(T // tm,),
        in_specs=[pl.BlockSpec((tm, D), lambda i: (i, 0)),
                  pl.BlockSpec((1, D), lambda i: (0, 0)),
                  pl.BlockSpec((1, D), lambda i: (0, 0)),
                  pl.BlockSpec((D, N), lambda i: (0, 0))],
        out_specs=pl.BlockSpec((tm, N), lambda i: (i, 0)),
        out_shape=jax.ShapeDtypeStruct((T, N), BF16),
        compiler_params=_cparams(("parallel",)),
    )(mem2d, g, b, w)


def _mem_attn_kernel(q_ref, kv_ref, o_ref):
    q = q_ref[0]
    k = kv_ref[0, :, :MEM_Q]
    v = kv_ref[0, :, MEM_Q:]
    lane = lax.broadcasted_iota(jnp.int32, q.shape, 1)
    o_acc = jnp.zeros(q.shape, F32)
    for h in range(MEM_HEADS):
        sel = (lane >= h * MEM_HD) & (lane < (h + 1) * MEM_HD)
        qz = jnp.where(sel, q, jnp.zeros_like(q))
        s = lax.dot_general(qz, k, (((1,), (1,)), ((), ())), preferred_element_type=F32)
        m = jnp.max(s, axis=-1, keepdims=True)
        p = jnp.exp(s - m)
        l = jnp.sum(p, axis=-1, keepdims=True)
        pv = jnp.dot(p.astype(BF16), v, preferred_element_type=F32) / l
        o_acc = jnp.where(sel, pv, o_acc)
    o_ref[0] = o_acc.astype(o_ref.dtype)


def _mem_attn(proj, kv, tm=512):
    B, S, _ = proj.shape
    M = kv.shape[1]
    tm = min(tm, S)
    qc = OFF_QM // MEM_Q
    return pl.pallas_call(
        _mem_attn_kernel,
        name="mem_attn",
        grid=(B, S // tm),
        in_specs=[pl.BlockSpec((1, tm, MEM_Q), lambda b, i: (b, i, qc)),
                  pl.BlockSpec((1, M, 2 * MEM_Q), lambda b, i: (b, 0, 0))],
        out_specs=pl.BlockSpec((1, tm, MEM_Q), lambda b, i: (b, i, 0)),
        out_shape=jax.ShapeDtypeStruct((B, S, MEM_Q), BF16),
        compiler_params=_cparams(("parallel", "parallel")),
    )(proj, kv)


HI16 = -65536


def _pack_halves(x):
    half = x.shape[1] // 2
    lo = lax.bitcast_convert_type(x[:, :half].astype(BF16).astype(F32), jnp.int32)
    hi = lax.bitcast_convert_type(x[:, half:].astype(BF16).astype(F32), jnp.int32)
    return lax.shift_right_logical(lo, 16) | (hi & HI16)


def _unpack_halves(r):
    lo = lax.bitcast_convert_type(lax.shift_left(r, 16), F32).astype(BF16)
    hi = lax.bitcast_convert_type(r & HI16, F32).astype(BF16)
    return lo, hi


def _merge_kernel(x_ref, od_ref, o0_ref, o1_ref, o2_ref, l0_ref, l1_ref, l2_ref, om_ref,
                  lng_ref, lnb_ref, wg_ref, bg_ref, wbd_ref, wbl_ref, wbm_ref, wo_ref, g1_ref, b1_ref,
                  x1_ref):
    D = x_ref.shape[1]
    xn = _ln(x_ref[...], lng_ref[...], lnb_ref[...])
    xb = xn.astype(BF16)
    l0, l1, l2 = l0_ref[...], l1_ref[...], l2_ref[...]
    mx = jnp.maximum(jnp.maximum(l0, l1), l2)
    e0, e1, e2 = jnp.exp(l0 - mx), jnp.exp(l1 - mx), jnp.exp(l2 - mx)
    ol = (e0 * o0_ref[...].astype(F32) + e1 * o1_ref[...].astype(F32) + e2 * o2_ref[...].astype(F32)) / (e0 + e1 + e2)
    branches = (jnp.dot(od_ref[...], wbd_ref[...], preferred_element_type=F32),
                jnp.dot(ol.astype(BF16), wbl_ref[...], preferred_element_type=F32),
                jnp.dot(om_ref[...], wbm_ref[...], preferred_element_type=F32))
    merged = jnp.zeros(xn.shape, F32)
    for n, br in enumerate(branches):
        cols = slice(n * D, (n + 1) * D)
        gate = jax.nn.sigmoid(jnp.dot(xb, wg_ref[:, cols], preferred_element_type=F32) + bg_ref[:, cols])
        merged = merged + gate * br
    y = DEEPNORM_ALPHA * xn + jnp.dot(merged.astype(BF16), wo_ref[...], preferred_element_type=F32)
    x1_ref[...] = _ln(y, g1_ref[...], b1_ref[...])


def _merge(x, od, odil, lses, om, lng, lnb, wg, bg, wbd, wbl, wbm, wo, g1, b1, tm=256):
    T, D = x.shape
    row = lambda w: pl.BlockSpec((tm, w), lambda i: (i, 0))
    full = lambda a: pl.BlockSpec(a.shape, lambda i: (0,) * a.ndim)
    params = (lng, lnb, wg, bg, wbd, wbl, wbm, wo, g1, b1)
    return pl.pallas_call(
        _merge_kernel,
        name="merge_out",
        grid=(T // tm,),
        in_specs=[row(D), row(D)] + [row(DIL_OUT)] * 7 + [full(a) for a in params],
        out_specs=row(D),
        out_shape=jax.ShapeDtypeStruct((T, D), F32),
        compiler_params=_cparams(("parallel",)),
    )(x, od, *odil, *lses, om, *params)


def _router_kernel(x_ref, w_ref, bias_ref, idx_ref, xr_ref):
    x = x_ref[...]
    logits = jnp.dot(x.astype(BF16), w_ref[...], preferred_element_type=F32)
    scores = jax.nn.sigmoid(logits)
    sel = scores + bias_ref[...]
    tm, E = sel.shape
    lane = lax.broadcasted_iota(jnp.int32, (tm, E), 1)
    olane = lax.broadcasted_iota(jnp.int32, idx_ref.shape, 1)
    idx_out = jnp.zeros(idx_ref.shape, jnp.int32)
    chosen = jnp.zeros((tm, E), jnp.bool_)
    for k in range(TOP_K):
        mx = jnp.max(sel, axis=-1, keepdims=True)
        idx = jnp.min(jnp.where(sel == mx, lane, E), axis=-1, keepdims=True)
        hit = lane == idx
        chosen = chosen | hit
        sel = jnp.where(hit, -jnp.inf, sel)
        idx_out = jnp.where(olane == k, idx, idx_out)
    picked = jnp.where(chosen, scores, 0.0)
    idx_ref[...] = idx_out
    rw = picked / jnp.sum(picked, axis=-1, keepdims=True) * ROUTED_SCALE
    half = x.shape[1] // 2
    xr_ref[:, :half] = _pack_halves(x)
    xr_ref[:, half:] = lax.bitcast_convert_type(rw, jnp.int32)


def _router(x1, w, bias, tm=512):
    T, D = x1.shape
    E = w.shape[1]
    return pl.pallas_call(
        _router_kernel,
        name="router",
        grid=(T // tm,),
        in_specs=[pl.BlockSpec((tm, D), lambda i: (i, 0)),
                  pl.BlockSpec((D, E), lambda i: (0, 0)),
                  pl.BlockSpec((1, E), lambda i: (0, 0))],
        out_specs=[pl.BlockSpec((tm, 128), lambda i: (i, 0)), pl.BlockSpec((tm, D // 2 + E), lambda i: (i, 0))],
        out_shape=[jax.ShapeDtypeStruct((T, 128), jnp.int32), jax.ShapeDtypeStruct((T, D // 2 + E), jnp.int32)],
        compiler_params=_cparams(("parallel",)),
    )(x1, w, bias)


def _moe_kernel(offs_ref, tok_ref, xr_ref, wg_ref, wu_ref, wd_ref, out_ref, rows_ref, y_ref, *, chunk, n_experts):
    tile = pl.program_id(0)
    step = pl.program_id(1)

    @pl.when(step == 0)
    def _():
        out_ref[...] = jnp.zeros(out_ref.shape, F32)

    @pl.when((tile == 0) & (step == 0))
    def _():
        rows_ref[...] = jnp.zeros(rows_ref.shape, rows_ref.dtype)

    half = wg_ref.shape[1] // 2
    lax.fori_loop(0, wg_ref.shape[0], functools.partial(
        _moe_one_expert, offs_ref=offs_ref, tok_ref=tok_ref, xr_ref=xr_ref, wg_ref=wg_ref, wu_ref=wu_ref,
        wd_ref=wd_ref, out_ref=out_ref, rows_ref=rows_ref, y_ref=y_ref, chunk=chunk, n_experts=n_experts,
        half=half, tile=tile, first=step * wg_ref.shape[0]), 0)


def _moe_one_expert(ee, carry, *, offs_ref, tok_ref, xr_ref, wg_ref, wu_ref, wd_ref, out_ref, rows_ref, y_ref,
                    chunk, n_experts, half, tile, first):
    e = first + ee
    start = offs_ref[tile * (n_experts + 1) + e]
    n = offs_ref[tile * (n_experts + 1) + e + 1] - start

    def one_chunk(c, carry):
        base = start + c * chunk
        cnt = jnp.minimum(chunk, n - c * chunk)

        for g in range(chunk // ROW_GROUP):
            @pl.when(g * ROW_GROUP < cnt)
            def _():
                for s in range(ROW_GROUP):
                    i = g * ROW_GROUP + s
                    t = tok_ref[0, 0, base + i]
                    rows_ref[i:i + 1, :] = xr_ref[pl.ds(t, 1), :]

        lo, hi = _unpack_halves(rows_ref[:, :half])
        g = (jnp.dot(lo, wg_ref[ee, :half], preferred_element_type=F32)
             + jnp.dot(hi, wg_ref[ee, half:], preferred_element_type=F32))
        u = (jnp.dot(lo, wu_ref[ee, :half], preferred_element_type=F32)
             + jnp.dot(hi, wu_ref[ee, half:], preferred_element_type=F32))
        wrow = lax.bitcast_convert_type(rows_ref[:, half:], F32)
        lane = lax.broadcasted_iota(jnp.int32, wrow.shape, 1)
        wcol = jnp.sum(jnp.where(lane == e, wrow, 0.0), axis=1, keepdims=True)
        wcol = jnp.where(lax.broadcasted_iota(jnp.int32, wcol.shape, 0) < cnt, wcol, 0.0)
        hid = (g * jax.nn.sigmoid(g)) * u * wcol
        y_ref[...] = jnp.dot(hid.astype(BF16), wd_ref[ee], preferred_element_type=F32)

        for g in range(chunk // ROW_GROUP):
            @pl.when((g + 1) * ROW_GROUP <= cnt)
            def _():
                toks = [tok_ref[0, 0, base + g * ROW_GROUP + s] for s in range(ROW_GROUP)]
                olds = [out_ref[pl.ds(t, 1), :] for t in toks]
                for s in range(ROW_GROUP):
                    i = g * ROW_GROUP + s
                    out_ref[pl.ds(toks[s], 1), :] = olds[s] + y_ref[i:i + 1, :]

        def scatter_row(i, carry):
            t = tok_ref[0, 0, base + i]
            out_ref[pl.ds(t, 1), :] += y_ref[pl.ds(i, 1), :]
            return carry

        lax.fori_loop(cnt // ROW_GROUP * ROW_GROUP, cnt, scatter_row, 0)
        return carry

    lax.fori_loop(0, (n + chunk - 1) // chunk, one_chunk, 0)
    return carry


def _moe_experts(offs, tok, xr, wg, wu, wd, tile):
    T, W = xr.shape
    E, D, F = wg.shape
    assert W == D // 2 + E
    nt = T // tile
    eps = math.gcd(E, EXPERTS_PER_STEP)
    grid_spec = pltpu.PrefetchScalarGridSpec(
        num_scalar_prefetch=1,
        grid=(nt, E // eps),
        in_specs=[pl.BlockSpec((1, 1, tok.shape[2]), lambda t, e, offs: (t, 0, 0), memory_space=pltpu.SMEM),
                  pl.BlockSpec((tile, W), lambda t, e, offs: (t, 0), pipeline_mode=pl.Buffered(1)),
                  pl.BlockSpec((eps, D, F), lambda t, e, offs: (e, 0, 0)),
                  pl.BlockSpec((eps, D, F), lambda t, e, offs: (e, 0, 0)),
                  pl.BlockSpec((eps, F, D), lambda t, e, offs: (e, 0, 0))],
        out_specs=pl.BlockSpec((tile, D), lambda t, e, offs: (t, 0), pipeline_mode=pl.Buffered(1)),
        scratch_shapes=[pltpu.VMEM((EXPERT_ROWS, W), jnp.int32), pltpu.VMEM((EXPERT_ROWS, D), F32)],
    )
    return pl.pallas_call(
        functools.partial(_moe_kernel, chunk=EXPERT_ROWS, n_experts=E),
        name="moe_experts",
        grid_spec=grid_spec,
        out_shape=jax.ShapeDtypeStruct((T, D), F32),
        compiler_params=_cparams(("arbitrary", "arbitrary")),
    )(offs, tok, xr, wg, wu, wd)


def _final_kernel(x1_ref, r_ref, wg_ref, wu_ref, wd_ref, g_ref, b_ref, o_ref):
    xb = x1_ref[...].astype(BF16)
    g = jnp.dot(xb, wg_ref[...], preferred_element_type=F32)
    u = jnp.dot(xb, wu_ref[...], preferred_element_type=F32)
    hid = (g * jax.nn.sigmoid(g)) * u
    shared = jnp.dot(hid.astype(BF16), wd_ref[...], preferred_element_type=F32)
    y = DEEPNORM_ALPHA * x1_ref[...] + (r_ref[...] + shared)
    o_ref[...] = _ln(y, g_ref[...], b_ref[...])


def _final(x1, routed, wg, wu, wd, g, b, tm=512):
    T, D = x1.shape
    row = pl.BlockSpec((tm, D), lambda i: (i, 0))
    full = lambda a: pl.BlockSpec(a.shape, lambda i: (0,) * a.ndim)
    params = (wg, wu, wd, g, b)
    return pl.pallas_call(
        _final_kernel,
        name="shared_final",
        grid=(T // tm,),
        in_specs=[row, row] + [full(a) for a in params],
        out_specs=row,
        out_shape=jax.ShapeDtypeStruct((T, D), F32),
        compiler_params=_cparams(("parallel",)),
    )(x1, routed, *params)


def _moe_routed(xr, top_idx, wg, wu, wd):
    T = xr.shape[0]
    E = wg.shape[0]
    tile = min(MOE_TILE, T)
    nt = T // tile
    e_tile = top_idx.reshape(nt, tile * TOP_K)
    order = jnp.argsort(e_tile, axis=1)
    e_sorted = jnp.take_along_axis(e_tile, order, axis=1)
    tok = jnp.pad((order // TOP_K).astype(jnp.int32), ((0, 0), (0, ROW_GROUP))).reshape(nt, 1, -1)
    bounds = jnp.arange(E + 1, dtype=jnp.int32)
    offs = jax.vmap(lambda row: jnp.searchsorted(row, bounds, side='left'))(e_sorted).astype(jnp.int32)
    return _moe_experts(offs.reshape(-1), tok, xr, wg, wu, wd, tile)


def _trunk(x, mem, p):
    B, S, D = x.shape
    T = B * S
    x2 = x.reshape(T, D)
    proj, d0, d1, d2, vt, norms = _ln_proj(x2, p['ln_in_g'], p['ln_in_b'], p['w_in'], p['w_vt'])
    proj = proj.reshape(B, S, D_MAIN)
    od = _diff_attn(proj, vt, norms, p['diff_slopes'], p['lam'], p['diff_subln_g'], p['lambda_init'])
    dil = [_dilated_group(d.reshape(B, S, D_DIL), p['dil_slopes'], g) for g, d in enumerate((d0, d1, d2))]
    kv = _mem_kv(mem.reshape(-1, D), p['ln_mem_g'], p['ln_mem_b'], p['w_mem_kv']).reshape(B, mem.shape[1], 2 * MEM_Q)
    om = _mem_attn(proj, kv)
    x1 = _merge(x2, od.reshape(T, DIFF_V), [o.reshape(T, DIL_OUT) for o, _ in dil],
                [l.reshape(T, DIL_OUT) for _, l in dil], om.reshape(T, MEM_Q),
                p['ln_in_g'], p['ln_in_b'], p['w_gate'], p['b_gate'], p['w_br_diff'], p['w_br_dil'],
                p['w_br_mem'], p['w_out'], p['ln1_g'], p['ln1_b'])
    idx, xr = _router(x1, p['w_router'], p['router_bias'])
    routed = _moe_routed(xr, idx[:, :TOP_K], p['w_e_gate'], p['w_e_up'], p['w_e_down'])
    y = _final(x1, routed, p['w_s_gate'], p['w_s_up'], p['w_s_down'], p['ln2_g'], p['ln2_b'])
    return y.reshape(B, S, D)


def kernel(x_prompt, x_sample, mem_prompt, mem_sample, ln_in_g, ln_in_b, ln_mem_g, ln_mem_b, w_in, w_gate, b_gate,
           lambda_q1, lambda_k1, lambda_q2, lambda_k2, diff_subln_g, w_br_diff, w_br_dil, w_br_mem, w_mem_kv,
           w_out, ln1_g, ln1_b, w_router, router_bias, w_e_gate, w_e_up, w_e_down, w_s_gate, w_s_up, w_s_down,
           ln2_g, ln2_b):
    l = 0
    lambda_init = 0.8 - 0.6 * math.exp(-0.3 * l)
    lam = (jnp.exp(jnp.sum(lambda_q1[l] * lambda_k1[l])) - jnp.exp(jnp.sum(lambda_q2[l] * lambda_k2[l]))
           + lambda_init).astype(F32).reshape(1)
    colscale = np.ones((D_IN,), np.float32)
    for off, width, hd in ((0, DIFF_QK, DIFF_DK), (SRC_QL, DIL_W, DIL_HD), (SRC_QM, MEM_Q, MEM_HD)):
        colscale[off:off + width] = hd ** -0.5
    colscale[:DIFF_QK] *= LOG2E
    w_all = (w_in[l] * colscale).astype(BF16)
    cols = lambda off, width: w_all[:, off:off + width]
    w_groups = [jnp.concatenate([cols(src + g * DIL_OUT, DIL_OUT) for src in (SRC_QL, SRC_KL, SRC_VL)], axis=1)
                for g in range(len(DIL_CONFIGS))]
    row = lambda a: a.reshape(1, -1).astype(F32)
    p = dict(
        ln_in_g=row(ln_in_g), ln_in_b=row(ln_in_b), ln_mem_g=row(ln_mem_g[l]), ln_mem_b=row(ln_mem_b[l]),
        w_in=jnp.concatenate([cols(0, SRC_VD), cols(SRC_QM, MEM_Q)] + w_groups, axis=1),
        w_vt=cols(SRC_VD, DIFF_V).T,
        w_gate=w_gate[l].astype(BF16), b_gate=row(b_gate[l]),
        lam=lam, lambda_init=lambda_init, diff_subln_g=diff_subln_g[l].reshape(-1, 1).astype(F32),
        diff_slopes=jnp.asarray(_alibi_slopes(DIFF_HEADS) * np.float32(LOG2E)), dil_slopes=jnp.asarray(_alibi_slopes(N_DIL_HEADS)),
        w_br_diff=w_br_diff[l].astype(BF16), w_br_dil=w_br_dil[l].astype(BF16), w_br_mem=w_br_mem[l].astype(BF16),
        w_mem_kv=w_mem_kv[l].astype(BF16), w_out=w_out[l].astype(BF16), ln1_g=row(ln1_g[l]), ln1_b=row(ln1_b[l]),
        w_router=w_router[l].astype(BF16), router_bias=row(router_bias[l]),
        w_e_gate=w_e_gate[l].astype(BF16), w_e_up=w_e_up[l].astype(BF16), w_e_down=w_e_down[l].astype(BF16),
        w_s_gate=w_s_gate[l].astype(BF16), w_s_up=w_s_up[l].astype(BF16), w_s_down=w_s_down[l].astype(BF16),
        ln2_g=row(ln2_g[l]), ln2_b=row(ln2_b[l]),
    )
    return (_trunk(x_prompt, mem_prompt, p), _trunk(x_sample, mem_sample, p))
```

```python
import functools
import math

import numpy as np
import jax
import jax.numpy as jnp
from jax import lax
from jax.experimental import pallas as pl
from jax.experimental.pallas import tpu as pltpu

F32 = jnp.float32
BF16 = jnp.bfloat16

DIFF_HEADS = 8
DIFF_DK = 64
DIFF_DV = 128
DIL_CONFIGS = ((128, 1), (512, 4), (2048, 16))
DIL_HEADS_PER_GROUP = 4
DIL_HD = 64
MEM_HEADS = 4
MEM_HD = 64
TOP_K = 8
ROUTED_SCALE = 2.5
LN_EPS = 1e-5
NEG_INF = -1e30
DEPTH = 1
DEEPNORM_ALPHA = (2 * DEPTH) ** 0.25
LOG2E = math.log2(math.e)

D_MODEL = 1024
DIFF_QK = DIFF_HEADS * 2 * DIFF_DK
DIFF_V = DIFF_HEADS * DIFF_DV
N_DIL_HEADS = len(DIL_CONFIGS) * DIL_HEADS_PER_GROUP
DIL_W = N_DIL_HEADS * DIL_HD
DIL_OUT = DIL_HEADS_PER_GROUP * DIL_HD
MEM_Q = MEM_HEADS * MEM_HD
D_IN = 2 * DIFF_QK + DIFF_V + 3 * DIL_W + MEM_Q
SRC_VD = 2 * DIFF_QK
SRC_QL = SRC_VD + DIFF_V
SRC_KL = SRC_QL + DIL_W
SRC_VL = SRC_KL + DIL_W
SRC_QM = SRC_VL + DIL_W
D_MAIN = 2 * DIFF_QK + MEM_Q
OFF_QD, OFF_KD, OFF_QM = 0, DIFF_QK, 2 * DIFF_QK
D_DIL = 3 * DIL_OUT
SEQ_TILE = 512
POS_SPLIT = 64
ONES_ROWS = 16
SKIP_LOG2 = 160.0
NORM_MARGIN = 1.02

VMEM_LIMIT = 56 * 1024 * 1024
EXPERT_ROWS = 160
EXPERTS_PER_STEP = 4
ROW_GROUP = 8
ROW_TILE_WORDS = 8 * 128
MOE_TILE = 4096


def _cparams(sem):
    return pltpu.CompilerParams(dimension_semantics=sem, vmem_limit_bytes=VMEM_LIMIT)


def _ln(x, g, b):
    mu = jnp.mean(x, axis=-1, keepdims=True)
    xc = x - mu
    var = jnp.mean(xc * xc, axis=-1, keepdims=True)
    return xc * lax.rsqrt(var + LN_EPS) * g + b


def _alibi_slopes(n):
    return np.asarray(2.0 ** (-8.0 * np.arange(1, n + 1) / n), dtype=np.float32)


def _ln_proj_kernel(x_ref, g_ref, b_ref, w_ref, wvt_ref, grp_ref, main_ref, d0_ref, d1_ref, d2_ref, vt_ref, nrm_ref,
                    xn_ref):
    xn_ref[...] = _ln(x_ref[...], g_ref[...], b_ref[...]).astype(BF16)
    col = 0
    for o_ref in (main_ref, d0_ref, d1_ref, d2_ref):
        for n in range(o_ref.shape[1] // D_DIL):
            o_ref[:, n * D_DIL:(n + 1) * D_DIL] = jnp.dot(
                xn_ref[...], w_ref[:, col:col + D_DIL], preferred_element_type=F32).astype(o_ref.dtype)
            col += D_DIL
    vt_ref[0] = lax.dot_general(wvt_ref[...], xn_ref[...], (((1,), (1,)), ((), ())),
                                preferred_element_type=F32).astype(vt_ref.dtype)
    qk = main_ref[:, :2 * DIFF_QK].astype(F32)
    sq = jnp.dot((qk * qk).astype(BF16), grp_ref[...], preferred_element_type=F32)
    nrm_ref[0] = jnp.max(sq, axis=0, keepdims=True)


def _ln_proj(x, g, b, w, wvt, tm=SEQ_TILE):
    T, D = x.shape
    N = w.shape[1]
    NV = wvt.shape[0]
    assert N == D_MAIN + 3 * D_DIL and D_MAIN % D_DIL == 0
    tok = lambda width: pl.BlockSpec((tm, width), lambda i: (i, 0))
    cols = lax.broadcasted_iota(jnp.int32, (2 * DIFF_QK, 128), 0)
    grp = (cols // DIFF_DK == lax.broadcasted_iota(jnp.int32, (2 * DIFF_QK, 128), 1)).astype(BF16)
    return pl.pallas_call(
        _ln_proj_kernel,
        name="ln_proj",
        grid=(T // tm,),
        in_specs=[pl.BlockSpec((tm, D), lambda i: (i, 0)),
                  pl.BlockSpec((1, D), lambda i: (0, 0)),
                  pl.BlockSpec((1, D), lambda i: (0, 0)),
                  pl.BlockSpec((D, N), lambda i: (0, 0)),
                  pl.BlockSpec((NV, D), lambda i: (0, 0)),
                  pl.BlockSpec((2 * DIFF_QK, 128), lambda i: (0, 0))],
        out_specs=[tok(D_MAIN), tok(D_DIL), tok(D_DIL), tok(D_DIL),
                   pl.BlockSpec((1, NV, tm), lambda i: (i, 0, 0)),
                   pl.BlockSpec((1, 1, 128), lambda i: (i, 0, 0))],
        out_shape=[jax.ShapeDtypeStruct((T, D_MAIN), BF16)] + [jax.ShapeDtypeStruct((T, D_DIL), BF16)] * 3
        + [jax.ShapeDtypeStruct((T // tm, NV, tm), BF16), jax.ShapeDtypeStruct((T // tm, 1, 128), F32)],
        scratch_shapes=[pltpu.VMEM((tm, D), BF16)],
        compiler_params=_cparams(("parallel",)),
    )(x, g, b, w, wvt, grp)


def _split3_bf16(x):
    hi = x.astype(BF16).astype(F32)
    mid = (x - hi).astype(BF16).astype(F32)
    return hi, mid, x - hi - mid


def _diff_attn_kernel(lo_ref, hi_ref, slope_ref, lam_ref, q_ref, k_ref, kpos_ref, vt_ref, g_ref, o_ref,
                      qx_ref, m_ref, acc_ref, sa_ref, sb_ref, *, tq, tk, lambda_init):
    h = pl.program_id(1)
    qi = pl.program_id(2)
    slope = slope_ref[h]
    q = q_ref[0]
    lane = lax.broadcasted_iota(jnp.int32, q.shape, 1)
    ipos = (qi * tq + lax.broadcasted_iota(jnp.int32, q.shape, 0)).astype(F32)
    a_hi, a_mid, a_lo = _split3_bf16(-slope * ipos)
    c = jnp.full(q.shape, slope, F32)
    c_hi = c.astype(BF16).astype(F32)
    c_lo = c - c_hi
    feats = (a_hi, a_mid, a_lo, POS_SPLIT * c_hi, c_hi, POS_SPLIT * c_lo, c_lo)
    ext = jnp.zeros(q.shape, F32)
    for n, f in enumerate(feats):
        ext = jnp.where(lane == n, f, ext)
    for comp in range(2):
        qz = jnp.where((lane >= comp * DIFF_DK) & (lane < (comp + 1) * DIFF_DK), q, jnp.zeros_like(q))
        qx_ref[comp, 0] = jnp.concatenate([qz, ext.astype(BF16)], axis=1)
        qx_ref[comp, 1] = jnp.concatenate([qz, (-ext).astype(BF16)], axis=1)
        qx_ref[comp, 2] = jnp.concatenate([qz, jnp.zeros_like(qz)], axis=1)
    m_ref[...] = jnp.full(m_ref.shape, NEG_INF, F32)
    acc_ref[...] = jnp.zeros(acc_ref.shape, F32)
    ones = jnp.where(lax.broadcasted_iota(jnp.int32, (ONES_ROWS, tk), 0) == 0, 1.0, 0.0).astype(BF16)

    def scores(j, side, comp):
        kx = jnp.concatenate([k_ref[0, j], kpos_ref[j]], axis=1)
        return lax.dot_general(kx, qx_ref[comp, side], (((1,), (1,)), ((), ())), preferred_element_type=F32)

    def softmax_pv(st, j, comp):
        vx = jnp.concatenate([vt_ref[0, j], ones], axis=0)
        m_old = m_ref[comp]
        m_new = jnp.maximum(m_old, jnp.max(st, axis=0, keepdims=True))
        pt = jnp.exp2(st - m_new).astype(BF16)
        acc_ref[comp] = jnp.exp2(m_old - m_new) * acc_ref[comp] + jnp.dot(vx, pt, preferred_element_type=F32)
        m_ref[comp] = m_new

    step = (pl.program_id(0) * pl.num_programs(1) + h) * pl.num_programs(2) + qi
    lo = lo_ref[step]
    n_pos = hi_ref[step] - lo + 1

    def key_block(b):
        j = lo + b - 1
        side = (j >= qi).astype(jnp.int32)
        return jnp.where(b == 0, qi, j + side), side

    def fill(b, dst_ref):
        j, side = key_block(b)
        for comp in range(2):
            dst_ref[comp] = scores(j, side, comp)

    def drain(b, src_ref):
        j, _ = key_block(b)
        for comp in range(2):
            softmax_pv(src_ref[comp], j, comp)

    def trip(t, carry):
        fill(2 * t + 1, sb_ref)
        drain(2 * t, sa_ref)
        fill(2 * t + 2, sa_ref)
        drain(2 * t + 1, sb_ref)
        return carry

    jpos = qi * tk + lax.broadcasted_iota(jnp.int32, (tk, tq), 0)
    ipos_t = qi * tq + lax.broadcasted_iota(jnp.int32, (tk, tq), 1)
    bias = slope * jnp.abs(jpos - ipos_t).astype(F32)
    for comp in range(2):
        sa_ref[comp] = scores(qi, 2, comp) - bias
    lax.fori_loop(0, (n_pos - 1) // 2, trip, 0)

    @pl.when(n_pos % 2 == 0)
    def _():
        fill(n_pos - 1, sb_ref)
        drain(n_pos - 2, sa_ref)
        drain(n_pos - 1, sb_ref)

    @pl.when(n_pos % 2 == 1)
    def _():
        drain(n_pos - 1, sa_ref)

    o0 = acc_ref[0, :DIFF_DV] / acc_ref[0, DIFF_DV:DIFF_DV + 1]
    o1 = acc_ref[1, :DIFF_DV] / acc_ref[1, DIFF_DV:DIFF_DV + 1]
    o = o0 - lam_ref[0] * o1
    ms = jnp.mean(o * o, axis=0, keepdims=True)
    o = o * lax.rsqrt(ms + LN_EPS) * g_ref[...] * (1.0 - lambda_init)
    o_ref[0] = o.T.astype(o_ref.dtype)


def _active_key_range(norms, slopes, tile):
    B, nb, _ = norms.shape
    nh = DIFF_HEADS * 2
    qn = (jnp.sqrt(norms[:, :, :nh]) * NORM_MARGIN).reshape(B, nb, DIFF_HEADS, 2)
    kn = (jnp.sqrt(norms[:, :, nh:2 * nh]) * NORM_MARGIN).reshape(B, nb, DIFF_HEADS, 2)
    blk = jnp.arange(nb, dtype=jnp.int32)
    gap = jnp.abs(blk[:, None] - blk[None, :])
    dmin = jnp.maximum((gap - 1) * tile + 1, 0).astype(F32)
    bound = (qn[:, :, None] * (kn[:, None, :] + kn[:, :, None])
             - (slopes[None, None, None, :, None] * dmin[None, :, :, None, None]))
    active = jnp.any(bound > -SKIP_LOG2, axis=-1) | (gap == 0)[None, :, :, None]
    lo = jnp.min(jnp.where(active, blk[None, None, :, None], nb), axis=2)
    hi = jnp.max(jnp.where(active, blk[None, None, :, None], -1), axis=2)
    to_bhi = lambda a: a.transpose(0, 2, 1).reshape(-1).astype(jnp.int32)
    return to_bhi(lo), to_bhi(hi)


def _diff_attn(proj, vt, norms, slopes, lam, g_col, lambda_init):
    B, S, _ = proj.shape
    tq = tk = min(SEQ_TILE, S)
    nk = S // tk
    pos = lax.broadcasted_iota(jnp.int32, (nk, tk, 128), 0) * tk + lax.broadcasted_iota(jnp.int32, (nk, tk, 128), 1)
    lane = lax.broadcasted_iota(jnp.int32, (nk, tk, 128), 2)
    kpos = jnp.where(lane < 3, 1, jnp.where((lane == 3) | (lane == 5), pos // POS_SPLIT,
                                            jnp.where((lane == 4) | (lane == 6), pos % POS_SPLIT, 0))).astype(BF16)
    lo, hi = _active_key_range(norms.reshape(B, nk, 128), slopes, tk)
    qb, kb = OFF_QD // 128, OFF_KD // 128
    grid_spec = pltpu.PrefetchScalarGridSpec(
        num_scalar_prefetch=2,
        grid=(B, DIFF_HEADS, S // tq),
        in_specs=[pl.BlockSpec(memory_space=pltpu.SMEM),
                  pl.BlockSpec(memory_space=pltpu.SMEM),
                  pl.BlockSpec((1, tq, 128), lambda b, h, i, lo, hi: (b, i, qb + h)),
                  pl.BlockSpec((1, nk, tk, 128), lambda b, h, i, lo, hi: (b, 0, 0, kb + h)),
                  pl.BlockSpec((nk, tk, 128), lambda b, h, i, lo, hi: (0, 0, 0)),
                  pl.BlockSpec((1, nk, DIFF_DV, tk), lambda b, h, i, lo, hi: (b, 0, h, 0)),
                  pl.BlockSpec((DIFF_DV, 1), lambda b, h, i, lo, hi: (0, 0))],
        out_specs=pl.BlockSpec((1, tq, 128), lambda b, h, i, lo, hi: (b, i, h)),
        scratch_shapes=[pltpu.VMEM((2, 3, tq, 256), BF16), pltpu.VMEM((2, 1, tq), F32),
                        pltpu.VMEM((2, DIFF_DV + ONES_ROWS, tq), F32),
                        pltpu.VMEM((2, tk, tq), F32), pltpu.VMEM((2, tk, tq), F32)],
    )
    return pl.pallas_call(
        functools.partial(_diff_attn_kernel, tq=tq, tk=tk, lambda_init=lambda_init),
        name="diff_attn",
        grid_spec=grid_spec,
        out_shape=jax.ShapeDtypeStruct((B, S, DIFF_V), BF16),
        compiler_params=_cparams(("parallel", "parallel", "parallel")),
    )(lo, hi, slopes, lam, proj, proj.reshape(B, nk, tk, -1), kpos, vt.reshape(B, nk, DIFF_V, tk), g_col)


def _dilated_kernel(slope_ref, q_ref, *refs, tl, half, n_kb, length, group, dilation):
    k_refs, v_refs, (o_ref, lse_ref) = refs[:n_kb], refs[n_kb:2 * n_kb], refs[2 * n_kb:]
    i = pl.program_id(2)
    q = q_ref[0]
    kcat = jnp.concatenate([r[0] for r in k_refs], axis=0)
    vcat = jnp.concatenate([r[0] for r in v_refs], axis=0)
    nkeys = n_kb * half
    nh = DIL_HEADS_PER_GROUP
    lane = lax.broadcasted_iota(jnp.int32, q.shape, 1)
    qstack = jnp.concatenate(
        [jnp.where((lane >= h * DIL_HD) & (lane < (h + 1) * DIL_HD), q, jnp.zeros_like(q)) for h in range(nh)], axis=0)
    row = lax.broadcasted_iota(jnp.int32, (nh * tl, 1), 0)
    slope = jnp.zeros((nh * tl, 1), F32)
    for h in range(nh):
        slope = jnp.where(row // tl == h, slope_ref[group * nh + h] * dilation, slope)
    qpos = i * tl + lax.broadcasted_iota(jnp.int32, (nh * tl, nkeys), 0) % tl
    kpos = i * tl - half + lax.broadcasted_iota(jnp.int32, (nh * tl, nkeys), 1)
    rel = jnp.abs(qpos - kpos)
    ok = (rel <= half) & (kpos >= 0) & (kpos < length)
    s = lax.dot_general(qstack, kcat, (((1,), (1,)), ((), ())), preferred_element_type=F32)
    s = jnp.where(ok, s - slope * rel.astype(F32), NEG_INF)
    m = jnp.max(s, axis=-1, keepdims=True)
    p = jnp.exp(s - m)
    l = jnp.sum(p, axis=-1, keepdims=True)
    pv = jnp.dot(p.astype(BF16), vcat, preferred_element_type=F32) / l
    lse = m + jnp.log(l)
    olane = lax.broadcasted_iota(jnp.int32, (tl, DIL_OUT), 1)
    o_acc = jnp.zeros((tl, DIL_OUT), F32)
    lse_acc = jnp.zeros((tl, DIL_OUT), F32)
    for h in range(nh):
        osel = (olane >= h * DIL_HD) & (olane < (h + 1) * DIL_HD)
        o_acc = jnp.where(osel, pv[h * tl:(h + 1) * tl], o_acc)
        lse_acc = jnp.where(osel, lse[h * tl:(h + 1) * tl], lse_acc)
    o_ref[0] = o_acc.astype(o_ref.dtype)
    lse_ref[0] = lse_acc


def _dilated_group(qkv, slopes, group, tl=128):
    B, S, _ = qkv.shape
    window, dil = DIL_CONFIGS[group]
    L = S // dil
    half = window // (2 * dil)
    tl = min(tl, L)
    assert tl % half == 0 and L % tl == 0
    per = tl // half
    n_kb = per + 2
    last_kb = L // half - 1
    pv = qkv.reshape(B, L, dil * D_DIL)
    nc = D_DIL // DIL_OUT
    qc, kc, vc = 0, 1, 2

    def kv_spec(col, n):
        def imap(b, r, i):
            return (b, jnp.clip(i * per - 1 + n, 0, last_kb), r * nc + col)
        return pl.BlockSpec((1, half, DIL_OUT), imap)

    q_spec = pl.BlockSpec((1, tl, DIL_OUT), lambda b, r, i: (b, i, r * nc + qc))
    out_spec = pl.BlockSpec((1, tl, DIL_OUT), lambda b, r, i: (b, i, r))
    o, lse = pl.pallas_call(
        functools.partial(_dilated_kernel, tl=tl, half=half, n_kb=n_kb, length=L, group=group, dilation=float(dil)),
        name=f"dilated_attn_{group}",
        grid=(B, dil, L // tl),
        in_specs=[pl.BlockSpec(memory_space=pltpu.SMEM), q_spec]
        + [kv_spec(kc, n) for n in range(n_kb)] + [kv_spec(vc, n) for n in range(n_kb)],
        out_specs=[out_spec, out_spec],
        out_shape=[jax.ShapeDtypeStruct((B, L, dil * DIL_OUT), BF16),
                   jax.ShapeDtypeStruct((B, L, dil * DIL_OUT), F32)],
        compiler_params=_cparams(("parallel", "parallel", "parallel")),
    )(slopes, *([pv] * (1 + 2 * n_kb)))
    return o.reshape(B, S, DIL_OUT), lse.reshape(B, S, DIL_OUT)


def _mem_kv_kernel(m_ref, g_ref, b_ref, w_ref, o_ref):
    mn = _ln(m_ref[...], g_ref[...], b_ref[...]).astype(BF16)
    o_ref[...] = jnp.dot(mn, w_ref[...], preferred_element_type=F32).astype(o_ref.dtype)


def _mem_kv(mem2d, g, b, w, tm=256):
    T, D = mem2d.shape
    N = w.shape[1]
    return pl.pallas_call(
        _mem_kv_kernel,
        name="mem_kv",
        grid=(T // tm,),
        in_specs=[pl.BlockSpec((tm, D), lambda i: (i, 0)),
                  pl.BlockSpec((1, D), lambda i: (0, 0)),
                  pl.BlockSpec((1, D), lambda i: (0, 0)),
                  pl.BlockSpec((D, N), lambda i: (0, 0))],
        out_specs=pl.BlockSpec((tm, N), lambda i: (i, 0)),
        out_shape=jax.ShapeDtypeStruct((T, N), BF16),
        compiler_params=_cparams(("parallel",)),
    )(mem2d, g, b, w)


def _mem_attn_kernel(q_ref, kv_ref, o_ref):
    q = q_ref[0]
    k = kv_ref[0, :, :MEM_Q]
    v = kv_ref[0, :, MEM_Q:]
    lane = lax.broadcasted_iota(jnp.int32, q.shape, 1)
    o_acc = jnp.zeros(q.shape, F32)
    for h in range(MEM_HEADS):
        sel = (lane >= h * MEM_HD) & (lane < (h + 1) * MEM_HD)
        qz = jnp.where(sel, q, jnp.zeros_like(q))
        s = lax.dot_general(qz, k, (((1,), (1,)), ((), ())), preferred_element_type=F32)
        m = jnp.max(s, axis=-1, keepdims=True)
        p = jnp.exp(s - m)
        l = jnp.sum(p, axis=-1, keepdims=True)
        pv = jnp.dot(p.astype(BF16), v, preferred_element_type=F32) / l
        o_acc = jnp.where(sel, pv, o_acc)
    o_ref[0] = o_acc.astype(o_ref.dtype)


def _mem_attn(proj, kv, tm=512):
    B, S, _ = proj.shape
    M = kv.shape[1]
    tm = min(tm, S)
    qc = OFF_QM // MEM_Q
    return pl.pallas_call(
        _mem_attn_kernel,
        name="mem_attn",
        grid=(B, S // tm),
        in_specs=[pl.BlockSpec((1, tm, MEM_Q), lambda b, i: (b, i, qc)),
                  pl.BlockSpec((1, M, 2 * MEM_Q), lambda b, i: (b, 0, 0))],
        out_specs=pl.BlockSpec((1, tm, MEM_Q), lambda b, i: (b, i, 0)),
        out_shape=jax.ShapeDtypeStruct((B, S, MEM_Q), BF16),
        compiler_params=_cparams(("parallel", "parallel")),
    )(proj, kv)


HI16 = -65536


def _pack_halves(x):
    half = x.shape[1] // 2
    lo = lax.bitcast_convert_type(x[:, :half].astype(BF16).astype(F32), jnp.int32)
    hi = lax.bitcast_convert_type(x[:, half:].astype(BF16).astype(F32), jnp.int32)
    return lax.shift_right_logical(lo, 16) | (hi & HI16)


def _unpack_halves(r):
    lo = lax.bitcast_convert_type(lax.shift_left(r, 16), F32).astype(BF16)
    hi = lax.bitcast_convert_type(r & HI16, F32).astype(BF16)
    return lo, hi


def _merge_kernel(x_ref, od_ref, o0_ref, o1_ref, o2_ref, l0_ref, l1_ref, l2_ref, om_ref,
                  lng_ref, lnb_ref, wg_ref, bg_ref, wbd_ref, wbl_ref, wbm_ref, wo_ref, g1_ref, b1_ref,
                  x1_ref):
    D = x_ref.shape[1]
    xn = _ln(x_ref[...], lng_ref[...], lnb_ref[...])
    xb = xn.astype(BF16)
    l0, l1, l2 = l0_ref[...], l1_ref[...], l2_ref[...]
    mx = jnp.maximum(jnp.maximum(l0, l1), l2)
    e0, e1, e2 = jnp.exp(l0 - mx), jnp.exp(l1 - mx), jnp.exp(l2 - mx)
    ol = (e0 * o0_ref[...].astype(F32) + e1 * o1_ref[...].astype(F32) + e2 * o2_ref[...].astype(F32)) / (e0 + e1 + e2)
    branches = (jnp.dot(od_ref[...], wbd_ref[...], preferred_element_type=F32),
                jnp.dot(ol.astype(BF16), wbl_ref[...], preferred_element_type=F32),
                jnp.dot(om_ref[...], wbm_ref[...], preferred_element_type=F32))
    merged = jnp.zeros(xn.shape, F32)
    for n, br in enumerate(branches):
        cols = slice(n * D, (n + 1) * D)
        gate = jax.nn.sigmoid(jnp.dot(xb, wg_ref[:, cols], preferred_element_type=F32) + bg_ref[:, cols])
        merged = merged + gate * br
    y = DEEPNORM_ALPHA * xn + jnp.dot(merged.astype(BF16), wo_ref[...], preferred_element_type=F32)
    x1_ref[...] = _ln(y, g1_ref[...], b1_ref[...])


def _merge(x, od, odil, lses, om, lng, lnb, wg, bg, wbd, wbl, wbm, wo, g1, b1, tm=256):
    T, D = x.shape
    row = lambda w: pl.BlockSpec((tm, w), lambda i: (i, 0))
    full = lambda a: pl.BlockSpec(a.shape, lambda i: (0,) * a.ndim)
    params = (lng, lnb, wg, bg, wbd, wbl, wbm, wo, g1, b1)
    return pl.pallas_call(
        _merge_kernel,
        name="merge_out",
        grid=(T // tm,),
        in_specs=[row(D), row(D)] + [row(DIL_OUT)] * 7 + [full(a) for a in params],
        out_specs=row(D),
        out_shape=jax.ShapeDtypeStruct((T, D), F32),
        compiler_params=_cparams(("parallel",)),
    )(x, od, *odil, *lses, om, *params)


def _router_kernel(x_ref, w_ref, bias_ref, idx_ref, xr_ref):
    x = x_ref[...]
    logits = jnp.dot(x.astype(BF16), w_ref[...], preferred_element_type=F32)
    scores = jax.nn.sigmoid(logits)
    sel = scores + bias_ref[...]
    tm, E = sel.shape
    lane = lax.broadcasted_iota(jnp.int32, (tm, E), 1)
    olane = lax.broadcasted_iota(jnp.int32, idx_ref.shape, 1)
    idx_out = jnp.zeros(idx_ref.shape, jnp.int32)
    chosen = jnp.zeros((tm, E), jnp.bool_)
    for k in range(TOP_K):
        mx = jnp.max(sel, axis=-1, keepdims=True)
        idx = jnp.min(jnp.where(sel == mx, lane, E), axis=-1, keepdims=True)
        hit = lane == idx
        chosen = chosen | hit
        sel = jnp.where(hit, -jnp.inf, sel)
        idx_out = jnp.where(olane == k, idx, idx_out)
    picked = jnp.where(chosen, scores, 0.0)
    idx_ref[...] = idx_out
    rw = picked / jnp.sum(picked, axis=-1, keepdims=True) * ROUTED_SCALE
    words = jnp.concatenate([_pack_halves(x), lax.bitcast_convert_type(rw, jnp.int32)], axis=1)
    pad = ROW_TILE_WORDS - words.shape[1]
    if pad:
        words = jnp.concatenate([words, jnp.zeros((tm, pad), jnp.int32)], axis=1)
    for s in range(8):
        xr_ref[pl.ds(s, tm, stride=8), :] = words[:, s * 128:(s + 1) * 128]


def _router(x1, w, bias, tm=512):
    T, D = x1.shape
    E = w.shape[1]
    return pl.pallas_call(
        _router_kernel,
        name="router",
        grid=(T // tm,),
        in_specs=[pl.BlockSpec((tm, D), lambda i: (i, 0)),
                  pl.BlockSpec((D, E), lambda i: (0, 0)),
                  pl.BlockSpec((1, E), lambda i: (0, 0))],
        out_specs=[pl.BlockSpec((tm, 128), lambda i: (i, 0)), pl.BlockSpec((tm * 8, 128), lambda i: (i, 0))],
        out_shape=[jax.ShapeDtypeStruct((T, 128), jnp.int32), jax.ShapeDtypeStruct((T * 8, 128), jnp.int32)],
        compiler_params=_cparams(("parallel",)),
    )(x1, w, bias)


def _moe_kernel(offs_ref, tok_ref, xr_ref, wg_ref, wu_ref, wd_ref, out_ref, rows_ref, y_ref, *, chunk, n_experts):
    tile = pl.program_id(0)
    step = pl.program_id(1)

    @pl.when(step == 0)
    def _():
        out_ref[...] = jnp.zeros(out_ref.shape, F32)

    @pl.when((tile == 0) & (step == 0))
    def _():
        rows_ref[...] = jnp.zeros(rows_ref.shape, rows_ref.dtype)

    half = wg_ref.shape[1] // 2
    lax.fori_loop(0, wg_ref.shape[0], functools.partial(
        _moe_one_expert, offs_ref=offs_ref, tok_ref=tok_ref, xr_ref=xr_ref, wg_ref=wg_ref, wu_ref=wu_ref,
        wd_ref=wd_ref, out_ref=out_ref, rows_ref=rows_ref, y_ref=y_ref, chunk=chunk, n_experts=n_experts,
        half=half, tile=tile, first=step * wg_ref.shape[0]), 0)


def _moe_one_expert(ee, carry, *, offs_ref, tok_ref, xr_ref, wg_ref, wu_ref, wd_ref, out_ref, rows_ref, y_ref,
                    chunk, n_experts, half, tile, first):
    e = first + ee
    start = offs_ref[tile * (n_experts + 1) + e]
    n = offs_ref[tile * (n_experts + 1) + e + 1] - start

    def one_chunk(c, carry):
        base = start + c * chunk
        cnt = jnp.minimum(chunk, n - c * chunk)

        for g in range(chunk // ROW_GROUP):
            @pl.when(g * ROW_GROUP < cnt)
            def _():
                for s in range(ROW_GROUP):
                    i = g * ROW_GROUP + s
                    t = tok_ref[0, 0, base + i]
                    rows_ref[i * 8:(i + 1) * 8, :] = xr_ref[t]

        def slab(ref, s):
            return ref[pl.ds(s, chunk, stride=8), :]

        n_act = half // 128
        n_rw = -(-n_experts // 128)
        lo, hi = _unpack_halves(jnp.concatenate([slab(rows_ref, s) for s in range(n_act)], axis=1))
        g = (jnp.dot(lo, wg_ref[ee, :half], preferred_element_type=F32)
             + jnp.dot(hi, wg_ref[ee, half:], preferred_element_type=F32))
        u = (jnp.dot(lo, wu_ref[ee, :half], preferred_element_type=F32)
             + jnp.dot(hi, wu_ref[ee, half:], preferred_element_type=F32))
        wrow = lax.bitcast_convert_type(
            jnp.concatenate([slab(rows_ref, n_act + s) for s in range(n_rw)], axis=1)[:, :n_experts], F32)
        lane = lax.broadcasted_iota(jnp.int32, wrow.shape, 1)
        wcol = jnp.sum(jnp.where(lane == e, wrow, 0.0), axis=1, keepdims=True)
        wcol = jnp.where(lax.broadcasted_iota(jnp.int32, wcol.shape, 0) < cnt, wcol, 0.0)
        hid = (g * jax.nn.sigmoid(g)) * u * wcol
        y = jnp.dot(hid.astype(BF16), wd_ref[ee], preferred_element_type=F32)
        for s in range(y.shape[1] // 128):
            y_ref[pl.ds(s, chunk, stride=8), :] = y[:, s * 128:(s + 1) * 128]

        for g in range(chunk // ROW_GROUP):
            @pl.when((g + 1) * ROW_GROUP <= cnt)
            def _():
                toks = [tok_ref[0, 0, base + g * ROW_GROUP + s] for s in range(ROW_GROUP)]
                olds = [out_ref[t] for t in toks]
                for s in range(ROW_GROUP):
                    i = g * ROW_GROUP + s
                    out_ref[toks[s]] = olds[s] + y_ref[i * 8:(i + 1) * 8, :]

        def scatter_row(i, carry):
            t = tok_ref[0, 0, base + i]
            out_ref[t] += y_ref[pl.ds(pl.multiple_of(i * 8, 8), 8), :]
            return carry

        lax.fori_loop(cnt // ROW_GROUP * ROW_GROUP, cnt, scatter_row, 0)
        return carry

    lax.fori_loop(0, (n + chunk - 1) // chunk, one_chunk, 0)
    return carry


def _moe_experts(offs, tok, xr, wg, wu, wd, tile):
    T = xr.shape[0]
    E, D, F = wg.shape
    assert D == ROW_TILE_WORDS and D // 2 + E <= ROW_TILE_WORDS and (D // 2) % 128 == 0
    nt = T // tile
    eps = math.gcd(E, EXPERTS_PER_STEP)
    grid_spec = pltpu.PrefetchScalarGridSpec(
        num_scalar_prefetch=1,
        grid=(nt, E // eps),
        in_specs=[pl.BlockSpec((1, 1, tok.shape[2]), lambda t, e, offs: (t, 0, 0), memory_space=pltpu.SMEM),
                  pl.BlockSpec((tile, 8, 128), lambda t, e, offs: (t, 0, 0), pipeline_mode=pl.Buffered(1)),
                  pl.BlockSpec((eps, D, F), lambda t, e, offs: (e, 0, 0)),
                  pl.BlockSpec((eps, D, F), lambda t, e, offs: (e, 0, 0)),
                  pl.BlockSpec((eps, F, D), lambda t, e, offs: (e, 0, 0))],
        out_specs=pl.BlockSpec((tile, 8, 128), lambda t, e, offs: (t, 0, 0), pipeline_mode=pl.Buffered(1)),
        scratch_shapes=[pltpu.VMEM((EXPERT_ROWS * 8, 128), jnp.int32), pltpu.VMEM((EXPERT_ROWS * 8, 128), F32)],
    )
    return pl.pallas_call(
        functools.partial(_moe_kernel, chunk=EXPERT_ROWS, n_experts=E),
        name="moe_experts",
        grid_spec=grid_spec,
        out_shape=jax.ShapeDtypeStruct((T, 8, 128), F32),
        compiler_params=_cparams(("arbitrary", "arbitrary")),
    )(offs, tok, xr, wg, wu, wd)


def _final_kernel(x1_ref, r_ref, wg_ref, wu_ref, wd_ref, g_ref, b_ref, o_ref):
    xb = x1_ref[...].astype(BF16)
    g = jnp.dot(xb, wg_ref[...], preferred_element_type=F32)
    u = jnp.dot(xb, wu_ref[...], preferred_element_type=F32)
    hid = (g * jax.nn.sigmoid(g)) * u
    shared = jnp.dot(hid.astype(BF16), wd_ref[...], preferred_element_type=F32)
    tm = x1_ref.shape[0]
    routed = jnp.concatenate([r_ref[pl.ds(s, tm, stride=8), :] for s in range(8)], axis=1)
    y = DEEPNORM_ALPHA * x1_ref[...] + (routed + shared)
    o_ref[...] = _ln(y, g_ref[...], b_ref[...])


def _final(x1, routed, wg, wu, wd, g, b, tm=512):
    T, D = x1.shape
    row = pl.BlockSpec((tm, D), lambda i: (i, 0))
    full = lambda a: pl.BlockSpec(a.shape, lambda i: (0,) * a.ndim)
    params = (wg, wu, wd, g, b)
    return pl.pallas_call(
        _final_kernel,
        name="shared_final",
        grid=(T // tm,),
        in_specs=[row, pl.BlockSpec((tm * 8, 128), lambda i: (i, 0))] + [full(a) for a in params],
        out_specs=row,
        out_shape=jax.ShapeDtypeStruct((T, D), F32),
        compiler_params=_cparams(("parallel",)),
    )(x1, routed.reshape(T * 8, 128), *params)


def _moe_routed(xr, top_idx, wg, wu, wd):
    T = xr.shape[0]
    E = wg.shape[0]
    tile = min(MOE_TILE, T)
    nt = T // tile
    e_tile = top_idx.reshape(nt, tile * TOP_K)
    order = jnp.argsort(e_tile, axis=1)
    e_sorted = jnp.take_along_axis(e_tile, order, axis=1)
    tok = jnp.pad((order // TOP_K).astype(jnp.int32), ((0, 0), (0, ROW_GROUP))).reshape(nt, 1, -1)
    bounds = jnp.arange(E + 1, dtype=jnp.int32)
    offs = jax.vmap(lambda row: jnp.searchsorted(row, bounds, side='left'))(e_sorted).astype(jnp.int32)
    return _moe_experts(offs.reshape(-1), tok, xr, wg, wu, wd, tile)


def _trunk(x, mem, p):
    B, S, D = x.shape
    T = B * S
    x2 = x.reshape(T, D)
    proj, d0, d1, d2, vt, norms = _ln_proj(x2, p['ln_in_g'], p['ln_in_b'], p['w_in'], p['w_vt'])
    proj = proj.reshape(B, S, D_MAIN)
    od = _diff_attn(proj, vt, norms, p['diff_slopes'], p['lam'], p['diff_subln_g'], p['lambda_init'])
    dil = [_dilated_group(d.reshape(B, S, D_DIL), p['dil_slopes'], g) for g, d in enumerate((d0, d1, d2))]
    kv = _mem_kv(mem.reshape(-1, D), p['ln_mem_g'], p['ln_mem_b'], p['w_mem_kv']).reshape(B, mem.shape[1], 2 * MEM_Q)
    om = _mem_attn(proj, kv)
    x1 = _merge(x2, od.reshape(T, DIFF_V), [o.reshape(T, DIL_OUT) for o, _ in dil],
                [l.reshape(T, DIL_OUT) for _, l in dil], om.reshape(T, MEM_Q),
                p['ln_in_g'], p['ln_in_b'], p['w_gate'], p['b_gate'], p['w_br_diff'], p['w_br_dil'],
                p['w_br_mem'], p['w_out'], p['ln1_g'], p['ln1_b'])
    idx, xr = _router(x1, p['w_router'], p['router_bias'])
    routed = _moe_routed(xr.reshape(T, 8, 128), idx[:, :TOP_K], p['w_e_gate'], p['w_e_up'], p['w_e_down'])
    y = _final(x1, routed, p['w_s_gate'], p['w_s_up'], p['w_s_down'], p['ln2_g'], p['ln2_b'])
    return y.reshape(B, S, D)


def kernel(x_prompt, x_sample, mem_prompt, mem_sample, ln_in_g, ln_in_b, ln_mem_g, ln_mem_b, w_in, w_gate, b_gate,
           lambda_q1, lambda_k1, lambda_q2, lambda_k2, diff_subln_g, w_br_diff, w_br_dil, w_br_mem, w_mem_kv,
           w_out, ln1_g, ln1_b, w_router, router_bias, w_e_gate, w_e_up, w_e_down, w_s_gate, w_s_up, w_s_down,
           ln2_g, ln2_b):
    l = 0
    lambda_init = 0.8 - 0.6 * math.exp(-0.3 * l)
    lam = (jnp.exp(jnp.sum(lambda_q1[l] * lambda_k1[l])) - jnp.exp(jnp.sum(lambda_q2[l] * lambda_k2[l]))
           + lambda_init).astype(F32).reshape(1)
    colscale = np.ones((D_IN,), np.float32)
    for off, width, hd in ((0, DIFF_QK, DIFF_DK), (SRC_QL, DIL_W, DIL_HD), (SRC_QM, MEM_Q, MEM_HD)):
        colscale[off:off + width] = hd ** -0.5
    colscale[:DIFF_QK] *= LOG2E
    w_all = (w_in[l] * colscale).astype(BF16)
    cols = lambda off, width: w_all[:, off:off + width]
    w_groups = [jnp.concatenate([cols(src + g * DIL_OUT, DIL_OUT) for src in (SRC_QL, SRC_KL, SRC_VL)], axis=1)
                for g in range(len(DIL_CONFIGS))]
    row = lambda a: a.reshape(1, -1).astype(F32)
    p = dict(
        ln_in_g=row(ln_in_g), ln_in_b=row(ln_in_b), ln_mem_g=row(ln_mem_g[l]), ln_mem_b=row(ln_mem_b[l]),
        w_in=jnp.concatenate([cols(0, SRC_VD), cols(SRC_QM, MEM_Q)] + w_groups, axis=1),
        w_vt=cols(SRC_VD, DIFF_V).T,
        w_gate=w_gate[l].astype(BF16), b_gate=row(b_gate[l]),
        lam=lam, lambda_init=lambda_init, diff_subln_g=diff_subln_g[l].reshape(-1, 1).astype(F32),
        diff_slopes=jnp.asarray(_alibi_slopes(DIFF_HEADS) * np.float32(LOG2E)), dil_slopes=jnp.asarray(_alibi_slopes(N_DIL_HEADS)),
        w_br_diff=w_br_diff[l].astype(BF16), w_br_dil=w_br_dil[l].astype(BF16), w_br_mem=w_br_mem[l].astype(BF16),
        w_mem_kv=w_mem_kv[l].astype(BF16), w_out=w_out[l].astype(BF16), ln1_g=row(ln1_g[l]), ln1_b=row(ln1_b[l]),
        w_router=w_router[l].astype(BF16), router_bias=row(router_bias[l]),
        w_e_gate=w_e_gate[l].astype(BF16), w_e_up=w_e_up[l].astype(BF16), w_e_down=w_e_down[l].astype(BF16),
        w_s_gate=w_s_gate[l].astype(BF16), w_s_up=w_s_up[l].astype(BF16), w_s_down=w_s_down[l].astype(BF16),
        ln2_g=row(ln2_g[l]), ln2_b=row(ln2_b[l]),
    )
    return (_trunk(x_prompt, mem_prompt, p), _trunk(x_sample, mem_sample, p))
```

```python
import functools
import math

import numpy as np
import jax
import jax.numpy as jnp
from jax import lax
from jax.experimental import pallas as pl
from jax.experimental.pallas import tpu as pltpu

F32 = jnp.float32
BF16 = jnp.bfloat16

DIFF_HEADS = 8
DIFF_DK = 64
DIFF_DV = 128
DIL_CONFIGS = ((128, 1), (512, 4), (2048, 16))
DIL_HEADS_PER_GROUP = 4
DIL_HD = 64
MEM_HEADS = 4
MEM_HD = 64
TOP_K = 8
ROUTED_SCALE = 2.5
LN_EPS = 1e-5
NEG_INF = -1e30
DEPTH = 1
DEEPNORM_ALPHA = (2 * DEPTH) ** 0.25
LOG2E = math.log2(math.e)

D_MODEL = 1024
DIFF_QK = DIFF_HEADS * 2 * DIFF_DK
DIFF_V = DIFF_HEADS * DIFF_DV
N_DIL_HEADS = len(DIL_CONFIGS) * DIL_HEADS_PER_GROUP
DIL_W = N_DIL_HEADS * DIL_HD
DIL_OUT = DIL_HEADS_PER_GROUP * DIL_HD
MEM_Q = MEM_HEADS * MEM_HD
D_IN = 2 * DIFF_QK + DIFF_V + 3 * DIL_W + MEM_Q
SRC_VD = 2 * DIFF_QK
SRC_QL = SRC_VD + DIFF_V
SRC_KL = SRC_QL + DIL_W
SRC_VL = SRC_KL + DIL_W
SRC_QM = SRC_VL + DIL_W
D_MAIN = 2 * DIFF_QK + MEM_Q
OFF_QD, OFF_KD, OFF_QM = 0, DIFF_QK, 2 * DIFF_QK
D_DIL = 3 * DIL_OUT
SEQ_TILE = 512
POS_SPLIT = 64
ONES_ROWS = 16
SKIP_LOG2 = 160.0
NORM_MARGIN = 1.02

VMEM_LIMIT = 56 * 1024 * 1024
EXPERT_ROWS = 160
EXPERTS_PER_STEP = 4
EXPERTS_TOGETHER = 2
SPARE_ROWS = 512
ROW_GROUP = 8
ROW_TILE_WORDS = 8 * 128
MOE_TILE = 4096


def _cparams(sem):
    return pltpu.CompilerParams(dimension_semantics=sem, vmem_limit_bytes=VMEM_LIMIT)


def _ln(x, g, b):
    mu = jnp.mean(x, axis=-1, keepdims=True)
    xc = x - mu
    var = jnp.mean(xc * xc, axis=-1, keepdims=True)
    return xc * lax.rsqrt(var + LN_EPS) * g + b


def _alibi_slopes(n):
    return np.asarray(2.0 ** (-8.0 * np.arange(1, n + 1) / n), dtype=np.float32)


def _ln_proj_kernel(x_ref, g_ref, b_ref, w_ref, wvt_ref, grp_ref, main_ref, d0_ref, d1_ref, d2_ref, vt_ref, nrm_ref,
                    xn_ref):
    xn_ref[...] = _ln(x_ref[...], g_ref[...], b_ref[...]).astype(BF16)
    col = 0
    for o_ref in (main_ref, d0_ref, d1_ref, d2_ref):
        for n in range(o_ref.shape[1] // D_DIL):
            o_ref[:, n * D_DIL:(n + 1) * D_DIL] = jnp.dot(
                xn_ref[...], w_ref[:, col:col + D_DIL], preferred_element_type=F32).astype(o_ref.dtype)
            col += D_DIL
    vt_ref[0] = lax.dot_general(wvt_ref[...], xn_ref[...], (((1,), (1,)), ((), ())),
                                preferred_element_type=F32).astype(vt_ref.dtype)
    qk = main_ref[:, :2 * DIFF_QK].astype(F32)
    sq = jnp.dot((qk * qk).astype(BF16), grp_ref[...], preferred_element_type=F32)
    nrm_ref[0] = jnp.max(sq, axis=0, keepdims=True)


def _ln_proj(x, g, b, w, wvt, tm=SEQ_TILE):
    T, D = x.shape
    N = w.shape[1]
    NV = wvt.shape[0]
    assert N == D_MAIN + 3 * D_DIL and D_MAIN % D_DIL == 0
    tok = lambda width: pl.BlockSpec((tm, width), lambda i: (i, 0))
    cols = lax.broadcasted_iota(jnp.int32, (2 * DIFF_QK, 128), 0)
    grp = (cols // DIFF_DK == lax.broadcasted_iota(jnp.int32, (2 * DIFF_QK, 128), 1)).astype(BF16)
    return pl.pallas_call(
        _ln_proj_kernel,
        name="ln_proj",
        grid=(T // tm,),
        in_specs=[pl.BlockSpec((tm, D), lambda i: (i, 0)),
                  pl.BlockSpec((1, D), lambda i: (0, 0)),
                  pl.BlockSpec((1, D), lambda i: (0, 0)),
                  pl.BlockSpec((D, N), lambda i: (0, 0)),
                  pl.BlockSpec((NV, D), lambda i: (0, 0)),
                  pl.BlockSpec((2 * DIFF_QK, 128), lambda i: (0, 0))],
        out_specs=[tok(D_MAIN), tok(D_DIL), tok(D_DIL), tok(D_DIL),
                   pl.BlockSpec((1, NV, tm), lambda i: (i, 0, 0)),
                   pl.BlockSpec((1, 1, 128), lambda i: (i, 0, 0))],
        out_shape=[jax.ShapeDtypeStruct((T, D_MAIN), BF16)] + [jax.ShapeDtypeStruct((T, D_DIL), BF16)] * 3
        + [jax.ShapeDtypeStruct((T // tm, NV, tm), BF16), jax.ShapeDtypeStruct((T // tm, 1, 128), F32)],
        scratch_shapes=[pltpu.VMEM((tm, D), BF16)],
        compiler_params=_cparams(("parallel",)),
    )(x, g, b, w, wvt, grp)


def _split3_bf16(x):
    hi = x.astype(BF16).astype(F32)
    mid = (x - hi).astype(BF16).astype(F32)
    return hi, mid, x - hi - mid


def _diff_attn_kernel(lo_ref, hi_ref, slope_ref, lam_ref, q_ref, k_ref, kpos_ref, vt_ref, g_ref, o_ref,
                      qx_ref, m_ref, acc_ref, sa_ref, sb_ref, *, tq, tk, lambda_init):
    h = pl.program_id(1)
    qi = pl.program_id(2)
    slope = slope_ref[h]
    q = q_ref[0]
    lane = lax.broadcasted_iota(jnp.int32, q.shape, 1)
    ipos = (qi * tq + lax.broadcasted_iota(jnp.int32, q.shape, 0)).astype(F32)
    a_hi, a_mid, a_lo = _split3_bf16(-slope * ipos)
    c = jnp.full(q.shape, slope, F32)
    c_hi = c.astype(BF16).astype(F32)
    c_lo = c - c_hi
    feats = (a_hi, a_mid, a_lo, POS_SPLIT * c_hi, c_hi, POS_SPLIT * c_lo, c_lo)
    ext = jnp.zeros(q.shape, F32)
    for n, f in enumerate(feats):
        ext = jnp.where(lane == n, f, ext)
    for comp in range(2):
        qz = jnp.where((lane >= comp * DIFF_DK) & (lane < (comp + 1) * DIFF_DK), q, jnp.zeros_like(q))
        qx_ref[comp, 0] = jnp.concatenate([qz, ext.astype(BF16)], axis=1)
        qx_ref[comp, 1] = jnp.concatenate([qz, (-ext).astype(BF16)], axis=1)
        qx_ref[comp, 2] = jnp.concatenate([qz, jnp.zeros_like(qz)], axis=1)
    m_ref[...] = jnp.full(m_ref.shape, NEG_INF, F32)
    acc_ref[...] = jnp.zeros(acc_ref.shape, F32)
    ones = jnp.where(lax.broadcasted_iota(jnp.int32, (ONES_ROWS, tk), 0) == 0, 1.0, 0.0).astype(BF16)

    def scores(j, side, comp):
        kx = jnp.concatenate([k_ref[0, j], kpos_ref[j]], axis=1)
        return lax.dot_general(kx, qx_ref[comp, side], (((1,), (1,)), ((), ())), preferred_element_type=F32)

    def softmax_pv(st, j, comp):
        vx = jnp.concatenate([vt_ref[0, j], ones], axis=0)
        m_old = m_ref[comp]
        m_new = jnp.maximum(m_old, jnp.max(st, axis=0, keepdims=True))
        pt = jnp.exp2(st - m_new).astype(BF16)
        acc_ref[comp] = jnp.exp2(m_old - m_new) * acc_ref[comp] + jnp.dot(vx, pt, preferred_element_type=F32)
        m_ref[comp] = m_new

    step = (pl.program_id(0) * pl.num_programs(1) + h) * pl.num_programs(2) + qi
    lo = lo_ref[step]
    n_pos = hi_ref[step] - lo + 1

    def key_block(b):
        j = lo + b - 1
        side = (j >= qi).astype(jnp.int32)
        return jnp.where(b == 0, qi, j + side), side

    def fill(b, dst_ref):
        j, side = key_block(b)
        for comp in range(2):
            dst_ref[comp] = scores(j, side, comp)

    def drain(b, src_ref):
        j, _ = key_block(b)
        for comp in range(2):
            softmax_pv(src_ref[comp], j, comp)

    def trip(t, carry):
        fill(2 * t + 1, sb_ref)
        drain(2 * t, sa_ref)
        fill(2 * t + 2, sa_ref)
        drain(2 * t + 1, sb_ref)
        return carry

    jpos = qi * tk + lax.broadcasted_iota(jnp.int32, (tk, tq), 0)
    ipos_t = qi * tq + lax.broadcasted_iota(jnp.int32, (tk, tq), 1)
    bias = slope * jnp.abs(jpos - ipos_t).astype(F32)
    for comp in range(2):
        sa_ref[comp] = scores(qi, 2, comp) - bias
    lax.fori_loop(0, (n_pos - 1) // 2, trip, 0)

    @pl.when(n_pos % 2 == 0)
    def _():
        fill(n_pos - 1, sb_ref)
        drain(n_pos - 2, sa_ref)
        drain(n_pos - 1, sb_ref)

    @pl.when(n_pos % 2 == 1)
    def _():
        drain(n_pos - 1, sa_ref)

    o0 = acc_ref[0, :DIFF_DV] / acc_ref[0, DIFF_DV:DIFF_DV + 1]
    o1 = acc_ref[1, :DIFF_DV] / acc_ref[1, DIFF_DV:DIFF_DV + 1]
    o = o0 - lam_ref[0] * o1
    ms = jnp.mean(o * o, axis=0, keepdims=True)
    o = o * lax.rsqrt(ms + LN_EPS) * g_ref[...] * (1.0 - lambda_init)
    o_ref[0] = o.T.astype(o_ref.dtype)


def _active_key_range(norms, slopes, tile):
    B, nb, _ = norms.shape
    nh = DIFF_HEADS * 2
    qn = (jnp.sqrt(norms[:, :, :nh]) * NORM_MARGIN).reshape(B, nb, DIFF_HEADS, 2)
    kn = (jnp.sqrt(norms[:, :, nh:2 * nh]) * NORM_MARGIN).reshape(B, nb, DIFF_HEADS, 2)
    blk = jnp.arange(nb, dtype=jnp.int32)
    gap = jnp.abs(blk[:, None] - blk[None, :])
    dmin = jnp.maximum((gap - 1) * tile + 1, 0).astype(F32)
    bound = (qn[:, :, None] * (kn[:, None, :] + kn[:, :, None])
             - (slopes[None, None, None, :, None] * dmin[None, :, :, None, None]))
    active = jnp.any(bound > -SKIP_LOG2, axis=-1) | (gap == 0)[None, :, :, None]
    lo = jnp.min(jnp.where(active, blk[None, None, :, None], nb), axis=2)
    hi = jnp.max(jnp.where(active, blk[None, None, :, None], -1), axis=2)
    to_bhi = lambda a: a.transpose(0, 2, 1).reshape(-1).astype(jnp.int32)
    return to_bhi(lo), to_bhi(hi)


def _diff_attn(proj, vt, norms, slopes, lam, g_col, lambda_init):
    B, S, _ = proj.shape
    tq = tk = min(SEQ_TILE, S)
    nk = S // tk
    pos = lax.broadcasted_iota(jnp.int32, (nk, tk, 128), 0) * tk + lax.broadcasted_iota(jnp.int32, (nk, tk, 128), 1)
    lane = lax.broadcasted_iota(jnp.int32, (nk, tk, 128), 2)
    kpos = jnp.where(lane < 3, 1, jnp.where((lane == 3) | (lane == 5), pos // POS_SPLIT,
                                            jnp.where((lane == 4) | (lane == 6), pos % POS_SPLIT, 0))).astype(BF16)
    lo, hi = _active_key_range(norms.reshape(B, nk, 128), slopes, tk)
    qb, kb = OFF_QD // 128, OFF_KD // 128
    grid_spec = pltpu.PrefetchScalarGridSpec(
        num_scalar_prefetch=2,
        grid=(B, DIFF_HEADS, S // tq),
        in_specs=[pl.BlockSpec(memory_space=pltpu.SMEM),
                  pl.BlockSpec(memory_space=pltpu.SMEM),
                  pl.BlockSpec((1, tq, 128), lambda b, h, i, lo, hi: (b, i, qb + h)),
                  pl.BlockSpec((1, nk, tk, 128), lambda b, h, i, lo, hi: (b, 0, 0, kb + h)),
                  pl.BlockSpec((nk, tk, 128), lambda b, h, i, lo, hi: (0, 0, 0)),
                  pl.BlockSpec((1, nk, DIFF_DV, tk), lambda b, h, i, lo, hi: (b, 0, h, 0)),
                  pl.BlockSpec((DIFF_DV, 1), lambda b, h, i, lo, hi: (0, 0))],
        out_specs=pl.BlockSpec((1, tq, 128), lambda b, h, i, lo, hi: (b, i, h)),
        scratch_shapes=[pltpu.VMEM((2, 3, tq, 256), BF16), pltpu.VMEM((2, 1, tq), F32),
                        pltpu.VMEM((2, DIFF_DV + ONES_ROWS, tq), F32),
                        pltpu.VMEM((2, tk, tq), F32), pltpu.VMEM((2, tk, tq), F32)],
    )
    return pl.pallas_call(
        functools.partial(_diff_attn_kernel, tq=tq, tk=tk, lambda_init=lambda_init),
        name="diff_attn",
        grid_spec=grid_spec,
        out_shape=jax.ShapeDtypeStruct((B, S, DIFF_V), BF16),
        compiler_params=_cparams(("parallel", "parallel", "parallel")),
    )(lo, hi, slopes, lam, proj, proj.reshape(B, nk, tk, -1), kpos, vt.reshape(B, nk, DIFF_V, tk), g_col)


def _dilated_kernel(slope_ref, q_ref, *refs, tl, half, n_kb, length, group, dilation):
    k_refs, v_refs, (o_ref, lse_ref) = refs[:n_kb], refs[n_kb:2 * n_kb], refs[2 * n_kb:]
    i = pl.program_id(2)
    q = q_ref[0]
    kcat = jnp.concatenate([r[0] for r in k_refs], axis=0)
    vcat = jnp.concatenate([r[0] for r in v_refs], axis=0)
    nkeys = n_kb * half
    nh = DIL_HEADS_PER_GROUP
    lane = lax.broadcasted_iota(jnp.int32, q.shape, 1)
    qstack = jnp.concatenate(
        [jnp.where((lane >= h * DIL_HD) & (lane < (h + 1) * DIL_HD), q, jnp.zeros_like(q)) for h in range(nh)], axis=0)
    row = lax.broadcasted_iota(jnp.int32, (nh * tl, 1), 0)
    slope = jnp.zeros((nh * tl, 1), F32)
    for h in range(nh):
        slope = jnp.where(row // tl == h, slope_ref[group * nh + h] * dilation, slope)
    qpos = i * tl + lax.broadcasted_iota(jnp.int32, (nh * tl, nkeys), 0) % tl
    kpos = i * tl - half + lax.broadcasted_iota(jnp.int32, (nh * tl, nkeys), 1)
    rel = jnp.abs(qpos - kpos)
    ok = (rel <= half) & (kpos >= 0) & (kpos < length)
    s = lax.dot_general(qstack, kcat, (((1,), (1,)), ((), ())), preferred_element_type=F32)
    s = jnp.where(ok, s - slope * rel.astype(F32), NEG_INF)
    m = jnp.max(s, axis=-1, keepdims=True)
    p = jnp.exp(s - m)
    l = jnp.sum(p, axis=-1, keepdims=True)
    pv = jnp.dot(p.astype(BF16), vcat, preferred_element_type=F32) / l
    lse = m + jnp.log(l)
    olane = lax.broadcasted_iota(jnp.int32, (tl, DIL_OUT), 1)
    o_acc = jnp.zeros((tl, DIL_OUT), F32)
    lse_acc = jnp.zeros((tl, DIL_OUT), F32)
    for h in range(nh):
        osel = (olane >= h * DIL_HD) & (olane < (h + 1) * DIL_HD)
        o_acc = jnp.where(osel, pv[h * tl:(h + 1) * tl], o_acc)
        lse_acc = jnp.where(osel, lse[h * tl:(h + 1) * tl], lse_acc)
    o_ref[0] = o_acc.astype(o_ref.dtype)
    lse_ref[0] = lse_acc


def _dilated_group(qkv, slopes, group, tl=128):
    B, S, _ = qkv.shape
    window, dil = DIL_CONFIGS[group]
    L = S // dil
    half = window // (2 * dil)
    tl = min(tl, L)
    assert tl % half == 0 and L % tl == 0
    per = tl // half
    n_kb = per + 2
    last_kb = L // half - 1
    pv = qkv.reshape(B, L, dil * D_DIL)
    nc = D_DIL // DIL_OUT
    qc, kc, vc = 0, 1, 2

    def kv_spec(col, n):
        def imap(b, r, i):
            return (b, jnp.clip(i * per - 1 + n, 0, last_kb), r * nc + col)
        return pl.BlockSpec((1, half, DIL_OUT), imap)

    q_spec = pl.BlockSpec((1, tl, DIL_OUT), lambda b, r, i: (b, i, r * nc + qc))
    out_spec = pl.BlockSpec((1, tl, DIL_OUT), lambda b, r, i: (b, i, r))
    o, lse = pl.pallas_call(
        functools.partial(_dilated_kernel, tl=tl, half=half, n_kb=n_kb, length=L, group=group, dilation=float(dil)),
        name=f"dilated_attn_{group}",
        grid=(B, dil, L // tl),
        in_specs=[pl.BlockSpec(memory_space=pltpu.SMEM), q_spec]
        + [kv_spec(kc, n) for n in range(n_kb)] + [kv_spec(vc, n) for n in range(n_kb)],
        out_specs=[out_spec, out_spec],
        out_shape=[jax.ShapeDtypeStruct((B, L, dil * DIL_OUT), BF16),
                   jax.ShapeDtypeStruct((B, L, dil * DIL_OUT), F32)],
        compiler_params=_cparams(("parallel", "parallel", "parallel")),
    )(slopes, *([pv] * (1 + 2 * n_kb)))
    return o.reshape(B, S, DIL_OUT), lse.reshape(B, S, DIL_OUT)


def _mem_kv_kernel(m_ref, g_ref, b_ref, w_ref, o_ref):
    mn = _ln(m_ref[...], g_ref[...], b_ref[...]).astype(BF16)
    o_ref[...] = jnp.dot(mn, w_ref[...], preferred_element_type=F32).astype(o_ref.dtype)


def _mem_kv(mem2d, g, b, w, tm=256):
    T, D = mem2d.shape
    N = w.shape[1]
    return pl.pallas_call(
        _mem_kv_kernel,
        name="mem_kv",
        grid=(T // tm,),
        in_specs=[pl.BlockSpec((tm, D), lambda i: (i, 0)),
                  pl.BlockSpec((1, D), lambda i: (0, 0)),
                  pl.BlockSpec((1, D), lambda i: (0, 0)),
                  pl.BlockSpec((D, N), lambda i: (0, 0))],
        out_specs=pl.BlockSpec((tm, N), lambda i: (i, 0)),
        out_shape=jax.ShapeDtypeStruct((T, N), BF16),
        compiler_params=_cparams(("parallel",)),
    )(mem2d, g, b, w)


def _mem_attn_kernel(q_ref, kv_ref, o_ref):
    q = q_ref[0]
    k = kv_ref[0, :, :MEM_Q]
    v = kv_ref[0, :, MEM_Q:]
    lane = lax.broadcasted_iota(jnp.int32, q.shape, 1)
    o_acc = jnp.zeros(q.shape, F32)
    for h in range(MEM_HEADS):
        sel = (lane >= h * MEM_HD) & (lane < (h + 1) * MEM_HD)
        qz = jnp.where(sel, q, jnp.zeros_like(q))
        s = lax.dot_general(qz, k, (((1,), (1,)), ((), ())), preferred_element_type=F32)
        m = jnp.max(s, axis=-1, keepdims=True)
        p = jnp.exp(s - m)
        l = jnp.sum(p, axis=-1, keepdims=True)
        pv = jnp.dot(p.astype(BF16), v, preferred_element_type=F32) / l
        o_acc = jnp.where(sel, pv, o_acc)
    o_ref[0] = o_acc.astype(o_ref.dtype)


def _mem_attn(proj, kv, tm=512):
    B, S, _ = proj.shape
    M = kv.shape[1]
    tm = min(tm, S)
    qc = OFF_QM // MEM_Q
    return pl.pallas_call(
        _mem_attn_kernel,
        name="mem_attn",
        grid=(B, S // tm),
        in_specs=[pl.BlockSpec((1, tm, MEM_Q), lambda b, i: (b, i, qc)),
                  pl.BlockSpec((1, M, 2 * MEM_Q), lambda b, i: (b, 0, 0))],
        out_specs=pl.BlockSpec((1, tm, MEM_Q), lambda b, i: (b, i, 0)),
        out_shape=jax.ShapeDtypeStruct((B, S, MEM_Q), BF16),
        compiler_params=_cparams(("parallel", "parallel")),
    )(proj, kv)


HI16 = -65536


def _pack_halves(x):
    half = x.shape[1] // 2
    lo = lax.bitcast_convert_type(x[:, :half].astype(BF16).astype(F32), jnp.int32)
    hi = lax.bitcast_convert_type(x[:, half:].astype(BF16).astype(F32), jnp.int32)
    return lax.shift_right_logical(lo, 16) | (hi & HI16)


def _unpack_halves(r):
    lo = lax.bitcast_convert_type(lax.shift_left(r, 16), F32).astype(BF16)
    hi = lax.bitcast_convert_type(r & HI16, F32).astype(BF16)
    return lo, hi


def _merge_kernel(x_ref, od_ref, o0_ref, o1_ref, o2_ref, l0_ref, l1_ref, l2_ref, om_ref,
                  lng_ref, lnb_ref, wg_ref, bg_ref, wbd_ref, wbl_ref, wbm_ref, wo_ref, g1_ref, b1_ref,
                  x1_ref):
    D = x_ref.shape[1]
    xn = _ln(x_ref[...], lng_ref[...], lnb_ref[...])
    xb = xn.astype(BF16)
    l0, l1, l2 = l0_ref[...], l1_ref[...], l2_ref[...]
    mx = jnp.maximum(jnp.maximum(l0, l1), l2)
    e0, e1, e2 = jnp.exp(l0 - mx), jnp.exp(l1 - mx), jnp.exp(l2 - mx)
    ol = (e0 * o0_ref[...].astype(F32) + e1 * o1_ref[...].astype(F32) + e2 * o2_ref[...].astype(F32)) / (e0 + e1 + e2)
    branches = (jnp.dot(od_ref[...], wbd_ref[...], preferred_element_type=F32),
                jnp.dot(ol.astype(BF16), wbl_ref[...], preferred_element_type=F32),
                jnp.dot(om_ref[...], wbm_ref[...], preferred_element_type=F32))
    merged = jnp.zeros(xn.shape, F32)
    for n, br in enumerate(branches):
        cols = slice(n * D, (n + 1) * D)
        gate = jax.nn.sigmoid(jnp.dot(xb, wg_ref[:, cols], preferred_element_type=F32) + bg_ref[:, cols])
        merged = merged + gate * br
    y = DEEPNORM_ALPHA * xn + jnp.dot(merged.astype(BF16), wo_ref[...], preferred_element_type=F32)
    x1_ref[...] = _ln(y, g1_ref[...], b1_ref[...])


def _merge(x, od, odil, lses, om, lng, lnb, wg, bg, wbd, wbl, wbm, wo, g1, b1, tm=256):
    T, D = x.shape
    row = lambda w: pl.BlockSpec((tm, w), lambda i: (i, 0))
    full = lambda a: pl.BlockSpec(a.shape, lambda i: (0,) * a.ndim)
    params = (lng, lnb, wg, bg, wbd, wbl, wbm, wo, g1, b1)
    return pl.pallas_call(
        _merge_kernel,
        name="merge_out",
        grid=(T // tm,),
        in_specs=[row(D), row(D)] + [row(DIL_OUT)] * 7 + [full(a) for a in params],
        out_specs=row(D),
        out_shape=jax.ShapeDtypeStruct((T, D), F32),
        compiler_params=_cparams(("parallel",)),
    )(x, od, *odil, *lses, om, *params)


def _router_kernel(x_ref, w_ref, bias_ref, idx_ref, xr_ref, cnt_ref):
    x = x_ref[...]
    logits = jnp.dot(x.astype(BF16), w_ref[...], preferred_element_type=F32)
    scores = jax.nn.sigmoid(logits)
    sel = scores + bias_ref[...]
    tm, E = sel.shape
    lane = lax.broadcasted_iota(jnp.int32, (tm, E), 1)
    olane = lax.broadcasted_iota(jnp.int32, idx_ref.shape, 1)
    idx_out = jnp.zeros(idx_ref.shape, jnp.int32)
    chosen = jnp.zeros((tm, E), jnp.bool_)
    for k in range(TOP_K):
        mx = jnp.max(sel, axis=-1, keepdims=True)
        idx = jnp.min(jnp.where(sel == mx, lane, E), axis=-1, keepdims=True)
        hit = lane == idx
        chosen = chosen | hit
        sel = jnp.where(hit, -jnp.inf, sel)
        idx_out = jnp.where(olane == k, idx, idx_out)
    picked = jnp.where(chosen, scores, 0.0)
    idx_ref[...] = idx_out
    cnt_ref[0] = jnp.sum(chosen.astype(jnp.int32), axis=0, keepdims=True)
    rw = picked / jnp.sum(picked, axis=-1, keepdims=True) * ROUTED_SCALE
    words = jnp.concatenate([_pack_halves(x), lax.bitcast_convert_type(rw, jnp.int32)], axis=1)
    pad = ROW_TILE_WORDS - words.shape[1]
    if pad:
        words = jnp.concatenate([words, jnp.zeros((tm, pad), jnp.int32)], axis=1)
    for s in range(8):
        xr_ref[pl.ds(s, tm, stride=8), :] = words[:, s * 128:(s + 1) * 128]


def _router(x1, w, bias, tm=512):
    T, D = x1.shape
    E = w.shape[1]
    return pl.pallas_call(
        _router_kernel,
        name="router",
        grid=(T // tm,),
        in_specs=[pl.BlockSpec((tm, D), lambda i: (i, 0)),
                  pl.BlockSpec((D, E), lambda i: (0, 0)),
                  pl.BlockSpec((1, E), lambda i: (0, 0))],
        out_specs=[pl.BlockSpec((tm, 128), lambda i: (i, 0)), pl.BlockSpec((tm * 8, 128), lambda i: (i, 0)),
                   pl.BlockSpec((1, 1, E), lambda i: (i, 0, 0))],
        out_shape=[jax.ShapeDtypeStruct((T, 128), jnp.int32), jax.ShapeDtypeStruct((T * 8, 128), jnp.int32),
                   jax.ShapeDtypeStruct((T // tm, 1, E), jnp.int32)],
        compiler_params=_cparams(("parallel",)),
    )(x1, w, bias)


def _moe_kernel(offs_ref, tok_ref, xr_ref, wg_ref, wu_ref, wd_ref, out_ref, rows_ref, y_ref, *, chunk, n_experts):
    tile = pl.program_id(0)
    step = pl.program_id(1)

    @pl.when(step == 0)
    def _():
        out_ref[...] = jnp.zeros(out_ref.shape, F32)

    @pl.when((tile == 0) & (step == 0))
    def _():
        rows_ref[...] = jnp.zeros(rows_ref.shape, rows_ref.dtype)

    half = wg_ref.shape[1] // 2
    n_together = rows_ref.shape[0]
    lax.fori_loop(0, wg_ref.shape[0] // n_together, functools.partial(
        _moe_expert_group, offs_ref=offs_ref, tok_ref=tok_ref, xr_ref=xr_ref, wg_ref=wg_ref, wu_ref=wu_ref,
        wd_ref=wd_ref, out_ref=out_ref, rows_ref=rows_ref, y_ref=y_ref, chunk=chunk, n_experts=n_experts,
        half=half, tile=tile, first=step * wg_ref.shape[0]), 0)


def _moe_expert_group(gg, carry, *, offs_ref, tok_ref, xr_ref, wg_ref, wu_ref, wd_ref, out_ref, rows_ref, y_ref,
                      chunk, n_experts, half, tile, first):
    n_together = rows_ref.shape[0]
    ees = [gg * n_together + k for k in range(n_together)]
    experts = [first + ee for ee in ees]
    starts = [offs_ref[tile * (n_experts + 1) + e] for e in experts]
    counts = [offs_ref[tile * (n_experts + 1) + e + 1] - st for e, st in zip(experts, starts)]
    n_act = half // 128
    n_rw = -(-n_experts // 128)

    def one_chunk(c, carry):
        cnts = [jnp.clip(n - c * chunk, 0, chunk) for n in counts]
        bases = [jnp.where(cnt > 0, st + c * chunk, 0) for st, cnt in zip(starts, cnts)]

        for k in range(n_together):
            for i in range(chunk):
                rows_ref[k, i * 8:(i + 1) * 8, :] = xr_ref[tok_ref[0, 0, bases[k] + i]]

        for k in range(n_together):
            ee = ees[k]

            def slab(s):
                return rows_ref[k, pl.ds(s, chunk, stride=8), :]

            lo, hi = _unpack_halves(jnp.concatenate([slab(s) for s in range(n_act)], axis=1))
            g = (jnp.dot(lo, wg_ref[ee, :half], preferred_element_type=F32)
                 + jnp.dot(hi, wg_ref[ee, half:], preferred_element_type=F32))
            u = (jnp.dot(lo, wu_ref[ee, :half], preferred_element_type=F32)
                 + jnp.dot(hi, wu_ref[ee, half:], preferred_element_type=F32))
            wrow = lax.bitcast_convert_type(
                jnp.concatenate([slab(n_act + s) for s in range(n_rw)], axis=1)[:, :n_experts], F32)
            lane = lax.broadcasted_iota(jnp.int32, wrow.shape, 1)
            wcol = jnp.sum(jnp.where(lane == experts[k], wrow, 0.0), axis=1, keepdims=True)
            wcol = jnp.where(lax.broadcasted_iota(jnp.int32, wcol.shape, 0) < cnts[k], wcol, 0.0)
            hid = (g * jax.nn.sigmoid(g)) * u * wcol
            y = jnp.dot(hid.astype(BF16), wd_ref[ee], preferred_element_type=F32)
            for s in range(y.shape[1] // 128):
                y_ref[k, pl.ds(s, chunk, stride=8), :] = y[:, s * 128:(s + 1) * 128]

        n_tok = out_ref.shape[0] - SPARE_ROWS
        for k in range(n_together):
            for g in range(chunk // ROW_GROUP):
                rows = []
                for s in range(ROW_GROUP):
                    i = g * ROW_GROUP + s
                    rows.append(jnp.where(i < cnts[k], tok_ref[0, 0, bases[k] + i], n_tok + s))
                olds = [out_ref[t] for t in rows]
                for s in range(ROW_GROUP):
                    i = g * ROW_GROUP + s
                    out_ref[rows[s]] = olds[s] + y_ref[k, i * 8:(i + 1) * 8, :]
        return carry

    n_max = functools.reduce(jnp.maximum, counts)
    lax.fori_loop(0, (n_max + chunk - 1) // chunk, one_chunk, 0)
    return carry


def _moe_experts(offs, tok, xr, wg, wu, wd, tile):
    T = xr.shape[0]
    E, D, F = wg.shape
    assert D == ROW_TILE_WORDS and D // 2 + E <= ROW_TILE_WORDS and (D // 2) % 128 == 0
    nt = T // tile
    eps = math.gcd(E, EXPERTS_PER_STEP)
    grid_spec = pltpu.PrefetchScalarGridSpec(
        num_scalar_prefetch=1,
        grid=(nt, E // eps),
        in_specs=[pl.BlockSpec((1, 1, tok.shape[2]), lambda t, e, offs: (t, 0, 0), memory_space=pltpu.SMEM),
                  pl.BlockSpec((tile, 8, 128), lambda t, e, offs: (t, 0, 0), pipeline_mode=pl.Buffered(1)),
                  pl.BlockSpec((eps, D, F), lambda t, e, offs: (e, 0, 0)),
                  pl.BlockSpec((eps, D, F), lambda t, e, offs: (e, 0, 0)),
                  pl.BlockSpec((eps, F, D), lambda t, e, offs: (e, 0, 0))],
        out_specs=pl.BlockSpec((tile + SPARE_ROWS, 8, 128), lambda t, e, offs: (t, 0, 0),
                               pipeline_mode=pl.Buffered(1)),
        scratch_shapes=[pltpu.VMEM((EXPERTS_TOGETHER, EXPERT_ROWS * 8, 128), jnp.int32),
                        pltpu.VMEM((EXPERTS_TOGETHER, EXPERT_ROWS * 8, 128), F32)],
    )
    return pl.pallas_call(
        functools.partial(_moe_kernel, chunk=EXPERT_ROWS, n_experts=E),
        name="moe_experts",
        grid_spec=grid_spec,
        out_shape=jax.ShapeDtypeStruct((nt * (tile + SPARE_ROWS), 8, 128), F32),
        compiler_params=_cparams(("arbitrary", "arbitrary")),
    )(offs, tok, xr, wg, wu, wd)


def _final_kernel(x1_ref, r_ref, wg_ref, wu_ref, wd_ref, g_ref, b_ref, o_ref):
    xb = x1_ref[...].astype(BF16)
    g = jnp.dot(xb, wg_ref[...], preferred_element_type=F32)
    u = jnp.dot(xb, wu_ref[...], preferred_element_type=F32)
    hid = (g * jax.nn.sigmoid(g)) * u
    shared = jnp.dot(hid.astype(BF16), wd_ref[...], preferred_element_type=F32)
    tm = x1_ref.shape[0]
    routed = jnp.concatenate([r_ref[pl.ds(s, tm, stride=8), :] for s in range(8)], axis=1)
    y = DEEPNORM_ALPHA * x1_ref[...] + (routed + shared)
    o_ref[...] = _ln(y, g_ref[...], b_ref[...])


def _final(x1, routed, moe_tile, wg, wu, wd, g, b, tm=SPARE_ROWS):
    T, D = x1.shape
    per = moe_tile // tm
    row = pl.BlockSpec((tm, D), lambda i: (i, 0))
    full = lambda a: pl.BlockSpec(a.shape, lambda i: (0,) * a.ndim)
    params = (wg, wu, wd, g, b)
    return pl.pallas_call(
        _final_kernel,
        name="shared_final",
        grid=(T // tm,),
        in_specs=[row, pl.BlockSpec((tm * 8, 128), lambda i: (i // per * (per + 1) + i % per, 0))]
        + [full(a) for a in params],
        out_specs=row,
        out_shape=jax.ShapeDtypeStruct((T, D), F32),
        compiler_params=_cparams(("parallel",)),
    )(x1, routed.reshape(-1, 128), *params)


def _moe_routed(xr, top_idx, counts, wg, wu, wd):
    T = xr.shape[0]
    E = wg.shape[0]
    tile = min(MOE_TILE, T)
    nt = T // tile
    n_assign = tile * TOP_K
    pos = lax.broadcasted_iota(jnp.int32, (nt, n_assign), 1)
    order = jnp.sort(top_idx.reshape(nt, n_assign) * n_assign + pos, axis=1) % n_assign
    tok = jnp.pad(order // TOP_K, ((0, 0), (0, EXPERT_ROWS))).reshape(nt, 1, -1)
    per_tile = counts.reshape(nt, -1, E).sum(axis=1)
    offs = jnp.concatenate([jnp.zeros((nt, 1), jnp.int32), jnp.cumsum(per_tile, axis=1, dtype=jnp.int32)], axis=1)
    return _moe_experts(offs.reshape(-1), tok, xr, wg, wu, wd, tile), tile


def _trunk(x, mem, p):
    B, S, D = x.shape
    T = B * S
    x2 = x.reshape(T, D)
    proj, d0, d1, d2, vt, norms = _ln_proj(x2, p['ln_in_g'], p['ln_in_b'], p['w_in'], p['w_vt'])
    proj = proj.reshape(B, S, D_MAIN)
    od = _diff_attn(proj, vt, norms, p['diff_slopes'], p['lam'], p['diff_subln_g'], p['lambda_init'])
    dil = [_dilated_group(d.reshape(B, S, D_DIL), p['dil_slopes'], g) for g, d in enumerate((d0, d1, d2))]
    kv = _mem_kv(mem.reshape(-1, D), p['ln_mem_g'], p['ln_mem_b'], p['w_mem_kv']).reshape(B, mem.shape[1], 2 * MEM_Q)
    om = _mem_attn(proj, kv)
    x1 = _merge(x2, od.reshape(T, DIFF_V), [o.reshape(T, DIL_OUT) for o, _ in dil],
                [l.reshape(T, DIL_OUT) for _, l in dil], om.reshape(T, MEM_Q),
                p['ln_in_g'], p['ln_in_b'], p['w_gate'], p['b_gate'], p['w_br_diff'], p['w_br_dil'],
                p['w_br_mem'], p['w_out'], p['ln1_g'], p['ln1_b'])
    idx, xr, counts = _router(x1, p['w_router'], p['router_bias'])
    routed, moe_tile = _moe_routed(xr.reshape(T, 8, 128), idx[:, :TOP_K], counts,
                                   p['w_e_gate'], p['w_e_up'], p['w_e_down'])
    y = _final(x1, routed, moe_tile, p['w_s_gate'], p['w_s_up'], p['w_s_down'], p['ln2_g'], p['ln2_b'])
    return y.reshape(B, S, D)


def kernel(x_prompt, x_sample, mem_prompt, mem_sample, ln_in_g, ln_in_b, ln_mem_g, ln_mem_b, w_in, w_gate, b_gate,
           lambda_q1, lambda_k1, lambda_q2, lambda_k2, diff_subln_g, w_br_diff, w_br_dil, w_br_mem, w_mem_kv,
           w_out, ln1_g, ln1_b, w_router, router_bias, w_e_gate, w_e_up, w_e_down, w_s_gate, w_s_up, w_s_down,
           ln2_g, ln2_b):
    l = 0
    lambda_init = 0.8 - 0.6 * math.exp(-0.3 * l)
    lam = (jnp.exp(jnp.sum(lambda_q1[l] * lambda_k1[l])) - jnp.exp(jnp.sum(lambda_q2[l] * lambda_k2[l]))
           + lambda_init).astype(F32).reshape(1)
    colscale = np.ones((D_IN,), np.float32)
    for off, width, hd in ((0, DIFF_QK, DIFF_DK), (SRC_QL, DIL_W, DIL_HD), (SRC_QM, MEM_Q, MEM_HD)):
        colscale[off:off + width] = hd ** -0.5
    colscale[:DIFF_QK] *= LOG2E
    w_all = (w_in[l] * colscale).astype(BF16)
    cols = lambda off, width: w_all[:, off:off + width]
    w_groups = [jnp.concatenate([cols(src + g * DIL_OUT, DIL_OUT) for src in (SRC_QL, SRC_KL, SRC_VL)], axis=1)
                for g in range(len(DIL_CONFIGS))]
    row = lambda a: a.reshape(1, -1).astype(F32)
    p = dict(
        ln_in_g=row(ln_in_g), ln_in_b=row(ln_in_b), ln_mem_g=row(ln_mem_g[l]), ln_mem_b=row(ln_mem_b[l]),
        w_in=jnp.concatenate([cols(0, SRC_VD), cols(SRC_QM, MEM_Q)] + w_groups, axis=1),
        w_vt=cols(SRC_VD, DIFF_V).T,
        w_gate=w_gate[l].astype(BF16), b_gate=row(b_gate[l]),
        lam=lam, lambda_init=lambda_init, diff_subln_g=diff_subln_g[l].reshape(-1, 1).astype(F32),
        diff_slopes=jnp.asarray(_alibi_slopes(DIFF_HEADS) * np.float32(LOG2E)), dil_slopes=jnp.asarray(_alibi_slopes(N_DIL_HEADS)),
        w_br_diff=w_br_diff[l].astype(BF16), w_br_dil=w_br_dil[l].astype(BF16), w_br_mem=w_br_mem[l].astype(BF16),
        w_mem_kv=w_mem_kv[l].astype(BF16), w_out=w_out[l].astype(BF16), ln1_g=row(ln1_g[l]), ln1_b=row(ln1_b[l]),
        w_router=w_router[l].astype(BF16), router_bias=row(router_bias[l]),
        w_e_gate=w_e_gate[l].astype(BF16), w_e_up=w_e_up[l].astype(BF16), w_e_down=w_e_down[l].astype(BF16),
        w_s_gate=w_s_gate[l].astype(BF16), w_s_up=w_s_up[l].astype(BF16), w_s_down=w_s_down[l].astype(BF16),
        ln2_g=row(ln2_g[l]), ln2_b=row(ln2_b[l]),
    )
    return (_trunk(x_prompt, mem_prompt, p), _trunk(x_sample, mem_sample, p))
```

```python
import functools
import math

import numpy as np
import jax
import jax.numpy as jnp
from jax import lax
from jax.experimental import pallas as pl
from jax.experimental.pallas import tpu as pltpu

F32 = jnp.float32
BF16 = jnp.bfloat16

DIFF_HEADS = 8
DIFF_DK = 64
DIFF_DV = 128
DIL_CONFIGS = ((128, 1), (512, 4), (2048, 16))
DIL_HEADS_PER_GROUP = 4
DIL_HD = 64
MEM_HEADS = 4
MEM_HD = 64
TOP_K = 8
ROUTED_SCALE = 2.5
LN_EPS = 1e-5
NEG_INF = -1e30
DEPTH = 1
DEEPNORM_ALPHA = (2 * DEPTH) ** 0.25
LOG2E = math.log2(math.e)

D_MODEL = 1024
DIFF_QK = DIFF_HEADS * 2 * DIFF_DK
DIFF_V = DIFF_HEADS * DIFF_DV
N_DIL_HEADS = len(DIL_CONFIGS) * DIL_HEADS_PER_GROUP
DIL_W = N_DIL_HEADS * DIL_HD
DIL_OUT = DIL_HEADS_PER_GROUP * DIL_HD
MEM_Q = MEM_HEADS * MEM_HD
D_IN = 2 * DIFF_QK + DIFF_V + 3 * DIL_W + MEM_Q
SRC_VD = 2 * DIFF_QK
SRC_QL = SRC_VD + DIFF_V
SRC_KL = SRC_QL + DIL_W
SRC_VL = SRC_KL + DIL_W
SRC_QM = SRC_VL + DIL_W
D_MAIN = 2 * DIFF_QK + MEM_Q
OFF_QD, OFF_KD, OFF_QM = 0, DIFF_QK, 2 * DIFF_QK
D_DIL = 3 * DIL_OUT
SEQ_TILE = 512
POS_SPLIT = 64
ONES_ROWS = 16
SKIP_LOG2 = 160.0
NORM_MARGIN = 1.02

VMEM_LIMIT = 56 * 1024 * 1024
EXPERT_ROWS = 160
EXPERTS_PER_STEP = 4
EXPERTS_TOGETHER = 4
SPARE_ROWS = 512
ROW_GROUP = 8
ROW_TILE_WORDS = 8 * 128
MOE_TILE = 4096


def _cparams(sem):
    return pltpu.CompilerParams(dimension_semantics=sem, vmem_limit_bytes=VMEM_LIMIT)


def _ln(x, g, b):
    mu = jnp.mean(x, axis=-1, keepdims=True)
    xc = x - mu
    var = jnp.mean(xc * xc, axis=-1, keepdims=True)
    return xc * lax.rsqrt(var + LN_EPS) * g + b


def _alibi_slopes(n):
    return np.asarray(2.0 ** (-8.0 * np.arange(1, n + 1) / n), dtype=np.float32)


def _ln_proj_kernel(x_ref, g_ref, b_ref, w_ref, wvt_ref, grp_ref, main_ref, d0_ref, d1_ref, d2_ref, vt_ref, nrm_ref,
                    xn_ref):
    xn_ref[...] = _ln(x_ref[...], g_ref[...], b_ref[...]).astype(BF16)
    col = 0
    for o_ref in (main_ref, d0_ref, d1_ref, d2_ref):
        for n in range(o_ref.shape[1] // D_DIL):
            o_ref[:, n * D_DIL:(n + 1) * D_DIL] = jnp.dot(
                xn_ref[...], w_ref[:, col:col + D_DIL], preferred_element_type=F32).astype(o_ref.dtype)
            col += D_DIL
    vt_ref[0] = lax.dot_general(wvt_ref[...], xn_ref[...], (((1,), (1,)), ((), ())),
                                preferred_element_type=F32).astype(vt_ref.dtype)
    qk = main_ref[:, :2 * DIFF_QK].astype(F32)
    sq = jnp.dot((qk * qk).astype(BF16), grp_ref[...], preferred_element_type=F32)
    nrm_ref[0] = jnp.max(sq, axis=0, keepdims=True)


def _ln_proj(x, g, b, w, wvt, tm=SEQ_TILE):
    T, D = x.shape
    N = w.shape[1]
    NV = wvt.shape[0]
    assert N == D_MAIN + 3 * D_DIL and D_MAIN % D_DIL == 0
    tok = lambda width: pl.BlockSpec((tm, width), lambda i: (i, 0))
    cols = lax.broadcasted_iota(jnp.int32, (2 * DIFF_QK, 128), 0)
    grp = (cols // DIFF_DK == lax.broadcasted_iota(jnp.int32, (2 * DIFF_QK, 128), 1)).astype(BF16)
    return pl.pallas_call(
        _ln_proj_kernel,
        name="ln_proj",
        grid=(T // tm,),
        in_specs=[pl.BlockSpec((tm, D), lambda i: (i, 0)),
                  pl.BlockSpec((1, D), lambda i: (0, 0)),
                  pl.BlockSpec((1, D), lambda i: (0, 0)),
                  pl.BlockSpec((D, N), lambda i: (0, 0)),
                  pl.BlockSpec((NV, D), lambda i: (0, 0)),
                  pl.BlockSpec((2 * DIFF_QK, 128), lambda i: (0, 0))],
        out_specs=[tok(D_MAIN), tok(D_DIL), tok(D_DIL), tok(D_DIL),
                   pl.BlockSpec((1, NV, tm), lambda i: (i, 0, 0)),
                   pl.BlockSpec((1, 1, 128), lambda i: (i, 0, 0))],
        out_shape=[jax.ShapeDtypeStruct((T, D_MAIN), BF16)] + [jax.ShapeDtypeStruct((T, D_DIL), BF16)] * 3
        + [jax.ShapeDtypeStruct((T // tm, NV, tm), BF16), jax.ShapeDtypeStruct((T // tm, 1, 128), F32)],
        scratch_shapes=[pltpu.VMEM((tm, D), BF16)],
        compiler_params=_cparams(("parallel",)),
    )(x, g, b, w, wvt, grp)


def _split3_bf16(x):
    hi = x.astype(BF16).astype(F32)
    mid = (x - hi).astype(BF16).astype(F32)
    return hi, mid, x - hi - mid


def _diff_attn_kernel(lo_ref, hi_ref, slope_ref, lam_ref, q_ref, k_ref, kpos_ref, vt_ref, g_ref, o_ref,
                      qx_ref, m_ref, acc_ref, sa_ref, sb_ref, *, tq, tk, lambda_init):
    h = pl.program_id(1)
    qi = pl.program_id(2)
    slope = slope_ref[h]
    q = q_ref[0]
    lane = lax.broadcasted_iota(jnp.int32, q.shape, 1)
    ipos = (qi * tq + lax.broadcasted_iota(jnp.int32, q.shape, 0)).astype(F32)
    a_hi, a_mid, a_lo = _split3_bf16(-slope * ipos)
    c = jnp.full(q.shape, slope, F32)
    c_hi = c.astype(BF16).astype(F32)
    c_lo = c - c_hi
    feats = (a_hi, a_mid, a_lo, POS_SPLIT * c_hi, c_hi, POS_SPLIT * c_lo, c_lo)
    ext = jnp.zeros(q.shape, F32)
    for n, f in enumerate(feats):
        ext = jnp.where(lane == n, f, ext)
    for comp in range(2):
        qz = jnp.where((lane >= comp * DIFF_DK) & (lane < (comp + 1) * DIFF_DK), q, jnp.zeros_like(q))
        qx_ref[comp, 0] = jnp.concatenate([qz, ext.astype(BF16)], axis=1)
        qx_ref[comp, 1] = jnp.concatenate([qz, (-ext).astype(BF16)], axis=1)
        qx_ref[comp, 2] = jnp.concatenate([qz, jnp.zeros_like(qz)], axis=1)
    m_ref[...] = jnp.full(m_ref.shape, NEG_INF, F32)
    acc_ref[...] = jnp.zeros(acc_ref.shape, F32)
    ones = jnp.where(lax.broadcasted_iota(jnp.int32, (ONES_ROWS, tk), 0) == 0, 1.0, 0.0).astype(BF16)

    def scores(j, side, comp):
        kx = jnp.concatenate([k_ref[0, j], kpos_ref[j]], axis=1)
        return lax.dot_general(kx, qx_ref[comp, side], (((1,), (1,)), ((), ())), preferred_element_type=F32)

    def softmax_pv(st, j, comp):
        vx = jnp.concatenate([vt_ref[0, j], ones], axis=0)
        m_old = m_ref[comp]
        m_new = jnp.maximum(m_old, jnp.max(st, axis=0, keepdims=True))
        pt = jnp.exp2(st - m_new).astype(BF16)
        acc_ref[comp] = jnp.exp2(m_old - m_new) * acc_ref[comp] + jnp.dot(vx, pt, preferred_element_type=F32)
        m_ref[comp] = m_new

    step = (pl.program_id(0) * pl.num_programs(1) + h) * pl.num_programs(2) + qi
    lo = lo_ref[step]
    n_pos = hi_ref[step] - lo + 1

    def key_block(b):
        j = lo + b - 1
        side = (j >= qi).astype(jnp.int32)
        return jnp.where(b == 0, qi, j + side), side

    def fill(b, dst_ref):
        j, side = key_block(b)
        for comp in range(2):
            dst_ref[comp] = scores(j, side, comp)

    def drain(b, src_ref):
        j, _ = key_block(b)
        for comp in range(2):
            softmax_pv(src_ref[comp], j, comp)

    def trip(t, carry):
        fill(2 * t + 1, sb_ref)
        drain(2 * t, sa_ref)
        fill(2 * t + 2, sa_ref)
        drain(2 * t + 1, sb_ref)
        return carry

    jpos = qi * tk + lax.broadcasted_iota(jnp.int32, (tk, tq), 0)
    ipos_t = qi * tq + lax.broadcasted_iota(jnp.int32, (tk, tq), 1)
    bias = slope * jnp.abs(jpos - ipos_t).astype(F32)
    for comp in range(2):
        sa_ref[comp] = scores(qi, 2, comp) - bias
    lax.fori_loop(0, (n_pos - 1) // 2, trip, 0)

    @pl.when(n_pos % 2 == 0)
    def _():
        fill(n_pos - 1, sb_ref)
        drain(n_pos - 2, sa_ref)
        drain(n_pos - 1, sb_ref)

    @pl.when(n_pos % 2 == 1)
    def _():
        drain(n_pos - 1, sa_ref)

    o0 = acc_ref[0, :DIFF_DV] / acc_ref[0, DIFF_DV:DIFF_DV + 1]
    o1 = acc_ref[1, :DIFF_DV] / acc_ref[1, DIFF_DV:DIFF_DV + 1]
    o = o0 - lam_ref[0] * o1
    ms = jnp.mean(o * o, axis=0, keepdims=True)
    o = o * lax.rsqrt(ms + LN_EPS) * g_ref[...] * (1.0 - lambda_init)
    o_ref[0] = o.T.astype(o_ref.dtype)


def _active_key_range(norms, slopes, tile):
    B, nb, _ = norms.shape
    nh = DIFF_HEADS * 2
    qn = (jnp.sqrt(norms[:, :, :nh]) * NORM_MARGIN).reshape(B, nb, DIFF_HEADS, 2)
    kn = (jnp.sqrt(norms[:, :, nh:2 * nh]) * NORM_MARGIN).reshape(B, nb, DIFF_HEADS, 2)
    blk = jnp.arange(nb, dtype=jnp.int32)
    gap = jnp.abs(blk[:, None] - blk[None, :])
    dmin = jnp.maximum((gap - 1) * tile + 1, 0).astype(F32)
    bound = (qn[:, :, None] * (kn[:, None, :] + kn[:, :, None])
             - (slopes[None, None, None, :, None] * dmin[None, :, :, None, None]))
    active = jnp.any(bound > -SKIP_LOG2, axis=-1) | (gap == 0)[None, :, :, None]
    lo = jnp.min(jnp.where(active, blk[None, None, :, None], nb), axis=2)
    hi = jnp.max(jnp.where(active, blk[None, None, :, None], -1), axis=2)
    to_bhi = lambda a: a.transpose(0, 2, 1).reshape(-1).astype(jnp.int32)
    return to_bhi(lo), to_bhi(hi)


def _diff_attn(proj, vt, norms, slopes, lam, g_col, lambda_init):
    B, S, _ = proj.shape
    tq = tk = min(SEQ_TILE, S)
    nk = S // tk
    pos = lax.broadcasted_iota(jnp.int32, (nk, tk, 128), 0) * tk + lax.broadcasted_iota(jnp.int32, (nk, tk, 128), 1)
    lane = lax.broadcasted_iota(jnp.int32, (nk, tk, 128), 2)
    kpos = jnp.where(lane < 3, 1, jnp.where((lane == 3) | (lane == 5), pos // POS_SPLIT,
                                            jnp.where((lane == 4) | (lane == 6), pos % POS_SPLIT, 0))).astype(BF16)
    lo, hi = _active_key_range(norms.reshape(B, nk, 128), slopes, tk)
    qb, kb = OFF_QD // 128, OFF_KD // 128
    grid_spec = pltpu.PrefetchScalarGridSpec(
        num_scalar_prefetch=2,
        grid=(B, DIFF_HEADS, S // tq),
        in_specs=[pl.BlockSpec(memory_space=pltpu.SMEM),
                  pl.BlockSpec(memory_space=pltpu.SMEM),
                  pl.BlockSpec((1, tq, 128), lambda b, h, i, lo, hi: (b, i, qb + h)),
                  pl.BlockSpec((1, nk, tk, 128), lambda b, h, i, lo, hi: (b, 0, 0, kb + h)),
                  pl.BlockSpec((nk, tk, 128), lambda b, h, i, lo, hi: (0, 0, 0)),
                  pl.BlockSpec((1, nk, DIFF_DV, tk), lambda b, h, i, lo, hi: (b, 0, h, 0)),
                  pl.BlockSpec((DIFF_DV, 1), lambda b, h, i, lo, hi: (0, 0))],
        out_specs=pl.BlockSpec((1, tq, 128), lambda b, h, i, lo, hi: (b, i, h)),
        scratch_shapes=[pltpu.VMEM((2, 3, tq, 256), BF16), pltpu.VMEM((2, 1, tq), F32),
                        pltpu.VMEM((2, DIFF_DV + ONES_ROWS, tq), F32),
                        pltpu.VMEM((2, tk, tq), F32), pltpu.VMEM((2, tk, tq), F32)],
    )
    return pl.pallas_call(
        functools.partial(_diff_attn_kernel, tq=tq, tk=tk, lambda_init=lambda_init),
        name="diff_attn",
        grid_spec=grid_spec,
        out_shape=jax.ShapeDtypeStruct((B, S, DIFF_V), BF16),
        compiler_params=_cparams(("parallel", "parallel", "parallel")),
    )(lo, hi, slopes, lam, proj, proj.reshape(B, nk, tk, -1), kpos, vt.reshape(B, nk, DIFF_V, tk), g_col)


def _dilated_kernel(slope_ref, q_ref, *refs, tl, half, n_kb, length, group, dilation):
    k_refs, v_refs, (o_ref, lse_ref) = refs[:n_kb], refs[n_kb:2 * n_kb], refs[2 * n_kb:]
    i = pl.program_id(2)
    q = q_ref[0]
    kcat = jnp.concatenate([r[0] for r in k_refs], axis=0)
    vcat = jnp.concatenate([r[0] for r in v_refs], axis=0)
    nkeys = n_kb * half
    nh = DIL_HEADS_PER_GROUP
    lane = lax.broadcasted_iota(jnp.int32, q.shape, 1)
    qstack = jnp.concatenate(
        [jnp.where((lane >= h * DIL_HD) & (lane < (h + 1) * DIL_HD), q, jnp.zeros_like(q)) for h in range(nh)], axis=0)
    row = lax.broadcasted_iota(jnp.int32, (nh * tl, 1), 0)
    slope = jnp.zeros((nh * tl, 1), F32)
    for h in range(nh):
        slope = jnp.where(row // tl == h, slope_ref[group * nh + h] * dilation, slope)
    qpos = i * tl + lax.broadcasted_iota(jnp.int32, (nh * tl, nkeys), 0) % tl
    kpos = i * tl - half + lax.broadcasted_iota(jnp.int32, (nh * tl, nkeys), 1)
    rel = jnp.abs(qpos - kpos)
    ok = (rel <= half) & (kpos >= 0) & (kpos < length)
    s = lax.dot_general(qstack, kcat, (((1,), (1,)), ((), ())), preferred_element_type=F32)
    s = jnp.where(ok, s - slope * rel.astype(F32), NEG_INF)
    m = jnp.max(s, axis=-1, keepdims=True)
    p = jnp.exp(s - m)
    l = jnp.sum(p, axis=-1, keepdims=True)
    pv = jnp.dot(p.astype(BF16), vcat, preferred_element_type=F32) / l
    lse = m + jnp.log(l)
    olane = lax.broadcasted_iota(jnp.int32, (tl, DIL_OUT), 1)
    o_acc = jnp.zeros((tl, DIL_OUT), F32)
    lse_acc = jnp.zeros((tl, DIL_OUT), F32)
    for h in range(nh):
        osel = (olane >= h * DIL_HD) & (olane < (h + 1) * DIL_HD)
        o_acc = jnp.where(osel, pv[h * tl:(h + 1) * tl], o_acc)
        lse_acc = jnp.where(osel, lse[h * tl:(h + 1) * tl], lse_acc)
    o_ref[0] = o_acc.astype(o_ref.dtype)
    lse_ref[0] = lse_acc


def _dilated_group(qkv, slopes, group, tl=128):
    B, S, _ = qkv.shape
    window, dil = DIL_CONFIGS[group]
    L = S // dil
    half = window // (2 * dil)
    tl = min(tl, L)
    assert tl % half == 0 and L % tl == 0
    per = tl // half
    n_kb = per + 2
    last_kb = L // half - 1
    pv = qkv.reshape(B, L, dil * D_DIL)
    nc = D_DIL // DIL_OUT
    qc, kc, vc = 0, 1, 2

    def kv_spec(col, n):
        def imap(b, r, i):
            return (b, jnp.clip(i * per - 1 + n, 0, last_kb), r * nc + col)
        return pl.BlockSpec((1, half, DIL_OUT), imap)

    q_spec = pl.BlockSpec((1, tl, DIL_OUT), lambda b, r, i: (b, i, r * nc + qc))
    out_spec = pl.BlockSpec((1, tl, DIL_OUT), lambda b, r, i: (b, i, r))
    o, lse = pl.pallas_call(
        functools.partial(_dilated_kernel, tl=tl, half=half, n_kb=n_kb, length=L, group=group, dilation=float(dil)),
        name=f"dilated_attn_{group}",
        grid=(B, dil, L // tl),
        in_specs=[pl.BlockSpec(memory_space=pltpu.SMEM), q_spec]
        + [kv_spec(kc, n) for n in range(n_kb)] + [kv_spec(vc, n) for n in range(n_kb)],
        out_specs=[out_spec, out_spec],
        out_shape=[jax.ShapeDtypeStruct((B, L, dil * DIL_OUT), BF16),
                   jax.ShapeDtypeStruct((B, L, dil * DIL_OUT), F32)],
        compiler_params=_cparams(("parallel", "parallel", "parallel")),
    )(slopes, *([pv] * (1 + 2 * n_kb)))
    return o.reshape(B, S, DIL_OUT), lse.reshape(B, S, DIL_OUT)


def _mem_kv_kernel(m_ref, g_ref, b_ref, w_ref, o_ref):
    mn = _ln(m_ref[...], g_ref[...], b_ref[...]).astype(BF16)
    o_ref[...] = jnp.dot(mn, w_ref[...], preferred_element_type=F32).astype(o_ref.dtype)


def _mem_kv(mem2d, g, b, w, tm=256):
    T, D = mem2d.shape
    N = w.shape[1]
    return pl.pallas_call(
        _mem_kv_kernel,
        name="mem_kv",
        grid=(T // tm,),
        in_specs=[pl.BlockSpec((tm, D), lambda i: (i, 0)),
                  pl.BlockSpec((1, D), lambda i: (0, 0)),
                  pl.BlockSpec((1, D), lambda i: (0, 0)),
                  pl.BlockSpec((D, N), lambda i: (0, 0))],
        out_specs=pl.BlockSpec((tm, N), lambda i: (i, 0)),
        out_shape=jax.ShapeDtypeStruct((T, N), BF16),
        compiler_params=_cparams(("parallel",)),
    )(mem2d, g, b, w)


def _mem_attn_kernel(q_ref, kv_ref, o_ref):
    q = q_ref[0]
    k = kv_ref[0, :, :MEM_Q]
    v = kv_ref[0, :, MEM_Q:]
    tm = q.shape[0]
    lane = lax.broadcasted_iota(jnp.int32, q.shape, 1)
    heads = [(lane >= h * MEM_HD) & (lane < (h + 1) * MEM_HD) for h in range(MEM_HEADS)]
    qstack = jnp.concatenate([jnp.where(sel, q, jnp.zeros_like(q)) for sel in heads], axis=0)
    s = lax.dot_general(qstack, k, (((1,), (1,)), ((), ())), preferred_element_type=F32)
    m = jnp.max(s, axis=-1, keepdims=True)
    p = jnp.exp(s - m)
    l = jnp.sum(p, axis=-1, keepdims=True)
    pv = jnp.dot(p.astype(BF16), v, preferred_element_type=F32) / l
    o_acc = jnp.zeros(q.shape, F32)
    for h, sel in enumerate(heads):
        o_acc = jnp.where(sel, pv[h * tm:(h + 1) * tm], o_acc)
    o_ref[0] = o_acc.astype(o_ref.dtype)


def _mem_attn(proj, kv, tm=512):
    B, S, _ = proj.shape
    M = kv.shape[1]
    tm = min(tm, S)
    qc = OFF_QM // MEM_Q
    return pl.pallas_call(
        _mem_attn_kernel,
        name="mem_attn",
        grid=(B, S // tm),
        in_specs=[pl.BlockSpec((1, tm, MEM_Q), lambda b, i: (b, i, qc)),
                  pl.BlockSpec((1, M, 2 * MEM_Q), lambda b, i: (b, 0, 0))],
        out_specs=pl.BlockSpec((1, tm, MEM_Q), lambda b, i: (b, i, 0)),
        out_shape=jax.ShapeDtypeStruct((B, S, MEM_Q), BF16),
        compiler_params=_cparams(("parallel", "parallel")),
    )(proj, kv)


HI16 = -65536


def _pack_halves(x):
    half = x.shape[1] // 2
    lo = lax.bitcast_convert_type(x[:, :half].astype(BF16).astype(F32), jnp.int32)
    hi = lax.bitcast_convert_type(x[:, half:].astype(BF16).astype(F32), jnp.int32)
    return lax.shift_right_logical(lo, 16) | (hi & HI16)


def _unpack_halves(r):
    lo = lax.bitcast_convert_type(lax.shift_left(r, 16), F32).astype(BF16)
    hi = lax.bitcast_convert_type(r & HI16, F32).astype(BF16)
    return lo, hi


def _merge_kernel(x_ref, od_ref, o0_ref, o1_ref, o2_ref, l0_ref, l1_ref, l2_ref, om_ref,
                  lng_ref, lnb_ref, wg_ref, bg_ref, wbd_ref, wbl_ref, wbm_ref, wo_ref, g1_ref, b1_ref,
                  x1_ref):
    D = x_ref.shape[1]
    xn = _ln(x_ref[...], lng_ref[...], lnb_ref[...])
    xb = xn.astype(BF16)
    l0, l1, l2 = l0_ref[...], l1_ref[...], l2_ref[...]
    mx = jnp.maximum(jnp.maximum(l0, l1), l2)
    e0, e1, e2 = jnp.exp(l0 - mx), jnp.exp(l1 - mx), jnp.exp(l2 - mx)
    ol = (e0 * o0_ref[...].astype(F32) + e1 * o1_ref[...].astype(F32) + e2 * o2_ref[...].astype(F32)) / (e0 + e1 + e2)
    branches = (jnp.dot(od_ref[...], wbd_ref[...], preferred_element_type=F32),
                jnp.dot(ol.astype(BF16), wbl_ref[...], preferred_element_type=F32),
                jnp.dot(om_ref[...], wbm_ref[...], preferred_element_type=F32))
    merged = jnp.zeros(xn.shape, F32)
    for n, br in enumerate(branches):
        cols = slice(n * D, (n + 1) * D)
        gate = jax.nn.sigmoid(jnp.dot(xb, wg_ref[:, cols], preferred_element_type=F32) + bg_ref[:, cols])
        merged = merged + gate * br
    y = DEEPNORM_ALPHA * xn + jnp.dot(merged.astype(BF16), wo_ref[...], preferred_element_type=F32)
    x1_ref[...] = _ln(y, g1_ref[...], b1_ref[...])


def _merge(x, od, odil, lses, om, lng, lnb, wg, bg, wbd, wbl, wbm, wo, g1, b1, tm=256):
    T, D = x.shape
    row = lambda w: pl.BlockSpec((tm, w), lambda i: (i, 0))
    full = lambda a: pl.BlockSpec(a.shape, lambda i: (0,) * a.ndim)
    params = (lng, lnb, wg, bg, wbd, wbl, wbm, wo, g1, b1)
    return pl.pallas_call(
        _merge_kernel,
        name="merge_out",
        grid=(T // tm,),
        in_specs=[row(D), row(D)] + [row(DIL_OUT)] * 7 + [full(a) for a in params],
        out_specs=row(D),
        out_shape=jax.ShapeDtypeStruct((T, D), F32),
        compiler_params=_cparams(("parallel",)),
    )(x, od, *odil, *lses, om, *params)


def _router_kernel(x_ref, w_ref, bias_ref, idx_ref, xr_ref, cnt_ref):
    x = x_ref[...]
    logits = jnp.dot(x.astype(BF16), w_ref[...], preferred_element_type=F32)
    scores = jax.nn.sigmoid(logits)
    sel = scores + bias_ref[...]
    tm, E = sel.shape
    lane = lax.broadcasted_iota(jnp.int32, (tm, E), 1)
    olane = lax.broadcasted_iota(jnp.int32, idx_ref.shape, 1)
    idx_out = jnp.zeros(idx_ref.shape, jnp.int32)
    chosen = jnp.zeros((tm, E), jnp.bool_)
    for k in range(TOP_K):
        mx = jnp.max(sel, axis=-1, keepdims=True)
        idx = jnp.min(jnp.where(sel == mx, lane, E), axis=-1, keepdims=True)
        hit = lane == idx
        chosen = chosen | hit
        sel = jnp.where(hit, -jnp.inf, sel)
        idx_out = jnp.where(olane == k, idx, idx_out)
    picked = jnp.where(chosen, scores, 0.0)
    idx_ref[...] = idx_out
    cnt_ref[0] = jnp.sum(chosen.astype(jnp.int32), axis=0, keepdims=True)
    rw = picked / jnp.sum(picked, axis=-1, keepdims=True) * ROUTED_SCALE
    words = jnp.concatenate([_pack_halves(x), lax.bitcast_convert_type(rw, jnp.int32)], axis=1)
    pad = ROW_TILE_WORDS - words.shape[1]
    if pad:
        words = jnp.concatenate([words, jnp.zeros((tm, pad), jnp.int32)], axis=1)
    for s in range(8):
        xr_ref[pl.ds(s, tm, stride=8), :] = words[:, s * 128:(s + 1) * 128]


def _router(x1, w, bias, tm=512):
    T, D = x1.shape
    E = w.shape[1]
    return pl.pallas_call(
        _router_kernel,
        name="router",
        grid=(T // tm,),
        in_specs=[pl.BlockSpec((tm, D), lambda i: (i, 0)),
                  pl.BlockSpec((D, E), lambda i: (0, 0)),
                  pl.BlockSpec((1, E), lambda i: (0, 0))],
        out_specs=[pl.BlockSpec((tm, 128), lambda i: (i, 0)), pl.BlockSpec((tm * 8, 128), lambda i: (i, 0)),
                   pl.BlockSpec((1, 1, E), lambda i: (i, 0, 0))],
        out_shape=[jax.ShapeDtypeStruct((T, 128), jnp.int32), jax.ShapeDtypeStruct((T * 8, 128), jnp.int32),
                   jax.ShapeDtypeStruct((T // tm, 1, E), jnp.int32)],
        compiler_params=_cparams(("parallel",)),
    )(x1, w, bias)


def _moe_kernel(offs_ref, tok_ref, xr_ref, wg_ref, wu_ref, wd_ref, out_ref, rows_ref, y_ref, *, chunk, n_experts):
    tile = pl.program_id(0)
    step = pl.program_id(1)

    @pl.when(step == 0)
    def _():
        out_ref[...] = jnp.zeros(out_ref.shape, F32)

    @pl.when((tile == 0) & (step == 0))
    def _():
        rows_ref[...] = jnp.zeros(rows_ref.shape, rows_ref.dtype)

    half = wg_ref.shape[1] // 2
    n_together = rows_ref.shape[0]
    lax.fori_loop(0, wg_ref.shape[0] // n_together, functools.partial(
        _moe_expert_group, offs_ref=offs_ref, tok_ref=tok_ref, xr_ref=xr_ref, wg_ref=wg_ref, wu_ref=wu_ref,
        wd_ref=wd_ref, out_ref=out_ref, rows_ref=rows_ref, y_ref=y_ref, chunk=chunk, n_experts=n_experts,
        half=half, tile=tile, first=step * wg_ref.shape[0]), 0)


def _moe_expert_group(gg, carry, *, offs_ref, tok_ref, xr_ref, wg_ref, wu_ref, wd_ref, out_ref, rows_ref, y_ref,
                      chunk, n_experts, half, tile, first):
    n_together = rows_ref.shape[0]
    ees = [gg * n_together + k for k in range(n_together)]
    experts = [first + ee for ee in ees]
    starts = [offs_ref[tile * (n_experts + 1) + e] for e in experts]
    counts = [offs_ref[tile * (n_experts + 1) + e + 1] - st for e, st in zip(experts, starts)]
    n_act = half // 128
    n_rw = -(-n_experts // 128)

    def one_chunk(c, carry):
        cnts = [jnp.clip(n - c * chunk, 0, chunk) for n in counts]
        bases = [jnp.where(cnt > 0, st + c * chunk, 0) for st, cnt in zip(starts, cnts)]

        for k in range(n_together):
            for i in range(chunk):
                rows_ref[k, i * 8:(i + 1) * 8, :] = xr_ref[tok_ref[0, 0, bases[k] + i]]

        for k in range(n_together):
            ee = ees[k]

            def slab(s):
                return rows_ref[k, pl.ds(s, chunk, stride=8), :]

            lo, hi = _unpack_halves(jnp.concatenate([slab(s) for s in range(n_act)], axis=1))
            g = (jnp.dot(lo, wg_ref[ee, :half], preferred_element_type=F32)
                 + jnp.dot(hi, wg_ref[ee, half:], preferred_element_type=F32))
            u = (jnp.dot(lo, wu_ref[ee, :half], preferred_element_type=F32)
                 + jnp.dot(hi, wu_ref[ee, half:], preferred_element_type=F32))
            wrow = lax.bitcast_convert_type(
                jnp.concatenate([slab(n_act + s) for s in range(n_rw)], axis=1)[:, :n_experts], F32)
            lane = lax.broadcasted_iota(jnp.int32, wrow.shape, 1)
            wcol = jnp.sum(jnp.where(lane == experts[k], wrow, 0.0), axis=1, keepdims=True)
            wcol = jnp.where(lax.broadcasted_iota(jnp.int32, wcol.shape, 0) < cnts[k], wcol, 0.0)
            hid = (g * jax.nn.sigmoid(g)) * u * wcol
            y = jnp.dot(hid.astype(BF16), wd_ref[ee], preferred_element_type=F32)
            for s in range(y.shape[1] // 128):
                y_ref[k, pl.ds(s, chunk, stride=8), :] = y[:, s * 128:(s + 1) * 128]

        n_tok = out_ref.shape[0] - SPARE_ROWS
        for k in range(n_together):
            for g in range(chunk // ROW_GROUP):
                rows = []
                for s in range(ROW_GROUP):
                    i = g * ROW_GROUP + s
                    rows.append(jnp.where(i < cnts[k], tok_ref[0, 0, bases[k] + i], n_tok + s))
                olds = [out_ref[t] for t in rows]
                for s in range(ROW_GROUP):
                    i = g * ROW_GROUP + s
                    out_ref[rows[s]] = olds[s] + y_ref[k, i * 8:(i + 1) * 8, :]
        return carry

    n_max = functools.reduce(jnp.maximum, counts)
    lax.fori_loop(0, (n_max + chunk - 1) // chunk, one_chunk, 0)
    return carry


def _moe_experts(offs, tok, xr, wg, wu, wd, tile):
    T = xr.shape[0]
    E, D, F = wg.shape
    assert D == ROW_TILE_WORDS and D // 2 + E <= ROW_TILE_WORDS and (D // 2) % 128 == 0
    nt = T // tile
    eps = math.gcd(E, EXPERTS_PER_STEP)
    grid_spec = pltpu.PrefetchScalarGridSpec(
        num_scalar_prefetch=1,
        grid=(nt, E // eps),
        in_specs=[pl.BlockSpec((1, 1, tok.shape[2]), lambda t, e, offs: (t, 0, 0), memory_space=pltpu.SMEM),
                  pl.BlockSpec((tile, 8, 128), lambda t, e, offs: (t, 0, 0), pipeline_mode=pl.Buffered(1)),
                  pl.BlockSpec((eps, D, F), lambda t, e, offs: (e, 0, 0)),
                  pl.BlockSpec((eps, D, F), lambda t, e, offs: (e, 0, 0)),
                  pl.BlockSpec((eps, F, D), lambda t, e, offs: (e, 0, 0))],
        out_specs=pl.BlockSpec((tile + SPARE_ROWS, 8, 128), lambda t, e, offs: (t, 0, 0),
                               pipeline_mode=pl.Buffered(1)),
        scratch_shapes=[pltpu.VMEM((EXPERTS_TOGETHER, EXPERT_ROWS * 8, 128), jnp.int32),
                        pltpu.VMEM((EXPERTS_TOGETHER, EXPERT_ROWS * 8, 128), F32)],
    )
    return pl.pallas_call(
        functools.partial(_moe_kernel, chunk=EXPERT_ROWS, n_experts=E),
        name="moe_experts",
        grid_spec=grid_spec,
        out_shape=jax.ShapeDtypeStruct((nt * (tile + SPARE_ROWS), 8, 128), F32),
        compiler_params=_cparams(("arbitrary", "arbitrary")),
    )(offs, tok, xr, wg, wu, wd)


def _final_kernel(x1_ref, r_ref, wg_ref, wu_ref, wd_ref, g_ref, b_ref, o_ref):
    xb = x1_ref[...].astype(BF16)
    g = jnp.dot(xb, wg_ref[...], preferred_element_type=F32)
    u = jnp.dot(xb, wu_ref[...], preferred_element_type=F32)
    hid = (g * jax.nn.sigmoid(g)) * u
    shared = jnp.dot(hid.astype(BF16), wd_ref[...], preferred_element_type=F32)
    tm = x1_ref.shape[0]
    routed = jnp.concatenate([r_ref[pl.ds(s, tm, stride=8), :] for s in range(8)], axis=1)
    y = DEEPNORM_ALPHA * x1_ref[...] + (routed + shared)
    o_ref[...] = _ln(y, g_ref[...], b_ref[...])


def _final(x1, routed, moe_tile, wg, wu, wd, g, b, tm=SPARE_ROWS):
    T, D = x1.shape
    per = moe_tile // tm
    row = pl.BlockSpec((tm, D), lambda i: (i, 0))
    full = lambda a: pl.BlockSpec(a.shape, lambda i: (0,) * a.ndim)
    params = (wg, wu, wd, g, b)
    return pl.pallas_call(
        _final_kernel,
        name="shared_final",
        grid=(T // tm,),
        in_specs=[row, pl.BlockSpec((tm * 8, 128), lambda i: (i // per * (per + 1) + i % per, 0))]
        + [full(a) for a in params],
        out_specs=row,
        out_shape=jax.ShapeDtypeStruct((T, D), F32),
        compiler_params=_cparams(("parallel",)),
    )(x1, routed.reshape(-1, 128), *params)


def _moe_routed(xr, top_idx, counts, wg, wu, wd):
    T = xr.shape[0]
    E = wg.shape[0]
    tile = min(MOE_TILE, T)
    nt = T // tile
    n_assign = tile * TOP_K
    pos = lax.broadcasted_iota(jnp.int32, (nt, n_assign), 1)
    order = jnp.sort(top_idx.reshape(nt, n_assign) * n_assign + pos, axis=1) % n_assign
    tok = jnp.pad(order // TOP_K, ((0, 0), (0, EXPERT_ROWS))).reshape(nt, 1, -1)
    per_tile = counts.reshape(nt, -1, E).sum(axis=1)
    offs = jnp.concatenate([jnp.zeros((nt, 1), jnp.int32), jnp.cumsum(per_tile, axis=1, dtype=jnp.int32)], axis=1)
    return _moe_experts(offs.reshape(-1), tok, xr, wg, wu, wd, tile), tile


def _trunk(x, mem, p):
    B, S, D = x.shape
    T = B * S
    x2 = x.reshape(T, D)
    proj, d0, d1, d2, vt, norms = _ln_proj(x2, p['ln_in_g'], p['ln_in_b'], p['w_in'], p['w_vt'])
    proj = proj.reshape(B, S, D_MAIN)
    od = _diff_attn(proj, vt, norms, p['diff_slopes'], p['lam'], p['diff_subln_g'], p['lambda_init'])
    dil = [_dilated_group(d.reshape(B, S, D_DIL), p['dil_slopes'], g) for g, d in enumerate((d0, d1, d2))]
    kv = _mem_kv(mem.reshape(-1, D), p['ln_mem_g'], p['ln_mem_b'], p['w_mem_kv']).reshape(B, mem.shape[1], 2 * MEM_Q)
    om = _mem_attn(proj, kv)
    x1 = _merge(x2, od.reshape(T, DIFF_V), [o.reshape(T, DIL_OUT) for o, _ in dil],
                [l.reshape(T, DIL_OUT) for _, l in dil], om.reshape(T, MEM_Q),
                p['ln_in_g'], p['ln_in_b'], p['w_gate'], p['b_gate'], p['w_br_diff'], p['w_br_dil'],
                p['w_br_mem'], p['w_out'], p['ln1_g'], p['ln1_b'])
    idx, xr, counts = _router(x1, p['w_router'], p['router_bias'])
    routed, moe_tile = _moe_routed(xr.reshape(T, 8, 128), idx[:, :TOP_K], counts,
                                   p['w_e_gate'], p['w_e_up'], p['w_e_down'])
    y = _final(x1, routed, moe_tile, p['w_s_gate'], p['w_s_up'], p['w_s_down'], p['ln2_g'], p['ln2_b'])
    return y.reshape(B, S, D)


def kernel(x_prompt, x_sample, mem_prompt, mem_sample, ln_in_g, ln_in_b, ln_mem_g, ln_mem_b, w_in, w_gate, b_gate,
           lambda_q1, lambda_k1, lambda_q2, lambda_k2, diff_subln_g, w_br_diff, w_br_dil, w_br_mem, w_mem_kv,
           w_out, ln1_g, ln1_b, w_router, router_bias, w_e_gate, w_e_up, w_e_down, w_s_gate, w_s_up, w_s_down,
           ln2_g, ln2_b):
    l = 0
    lambda_init = 0.8 - 0.6 * math.exp(-0.3 * l)
    lam = (jnp.exp(jnp.sum(lambda_q1[l] * lambda_k1[l])) - jnp.exp(jnp.sum(lambda_q2[l] * lambda_k2[l]))
           + lambda_init).astype(F32).reshape(1)
    colscale = np.ones((D_IN,), np.float32)
    for off, width, hd in ((0, DIFF_QK, DIFF_DK), (SRC_QL, DIL_W, DIL_HD), (SRC_QM, MEM_Q, MEM_HD)):
        colscale[off:off + width] = hd ** -0.5
    colscale[:DIFF_QK] *= LOG2E
    w_all = (w_in[l] * colscale).astype(BF16)
    cols = lambda off, width: w_all[:, off:off + width]
    w_groups = [jnp.concatenate([cols(src + g * DIL_OUT, DIL_OUT) for src in (SRC_QL, SRC_KL, SRC_VL)], axis=1)
                for g in range(len(DIL_CONFIGS))]
    row = lambda a: a.reshape(1, -1).astype(F32)
    p = dict(
        ln_in_g=row(ln_in_g), ln_in_b=row(ln_in_b), ln_mem_g=row(ln_mem_g[l]), ln_mem_b=row(ln_mem_b[l]),
        w_in=jnp.concatenate([cols(0, SRC_VD), cols(SRC_QM, MEM_Q)] + w_groups, axis=1),
        w_vt=cols(SRC_VD, DIFF_V).T,
        w_gate=w_gate[l].astype(BF16), b_gate=row(b_gate[l]),
        lam=lam, lambda_init=lambda_init, diff_subln_g=diff_subln_g[l].reshape(-1, 1).astype(F32),
        diff_slopes=jnp.asarray(_alibi_slopes(DIFF_HEADS) * np.float32(LOG2E)), dil_slopes=jnp.asarray(_alibi_slopes(N_DIL_HEADS)),
        w_br_diff=w_br_diff[l].astype(BF16), w_br_dil=w_br_dil[l].astype(BF16), w_br_mem=w_br_mem[l].astype(BF16),
        w_mem_kv=w_mem_kv[l].astype(BF16), w_out=w_out[l].astype(BF16), ln1_g=row(ln1_g[l]), ln1_b=row(ln1_b[l]),
        w_router=w_router[l].astype(BF16), router_bias=row(router_bias[l]),
        w_e_gate=w_e_gate[l].astype(BF16), w_e_up=w_e_up[l].astype(BF16), w_e_down=w_e_down[l].astype(BF16),
        w_s_gate=w_s_gate[l].astype(BF16), w_s_up=w_s_up[l].astype(BF16), w_s_down=w_s_down[l].astype(BF16),
        ln2_g=row(ln2_g[l]), ln2_b=row(ln2_b[l]),
    )
    return (_trunk(x_prompt, mem_prompt, p), _trunk(x_sample, mem_sample, p))
```

```python
import functools
import math

import numpy as np
import jax
import jax.numpy as jnp
from jax import lax
from jax.experimental import pallas as pl
from jax.experimental.pallas import tpu as pltpu

F32 = jnp.float32
BF16 = jnp.bfloat16

DIFF_HEADS = 8
DIFF_DK = 64
DIFF_DV = 128
DIL_CONFIGS = ((128, 1), (512, 4), (2048, 16))
DIL_HEADS_PER_GROUP = 4
DIL_HD = 64
MEM_HEADS = 4
MEM_HD = 64
TOP_K = 8
ROUTED_SCALE = 2.5
LN_EPS = 1e-5
NEG_INF = -1e30
DEPTH = 1
DEEPNORM_ALPHA = (2 * DEPTH) ** 0.25
LOG2E = math.log2(math.e)

D_MODEL = 1024
DIFF_QK = DIFF_HEADS * 2 * DIFF_DK
DIFF_V = DIFF_HEADS * DIFF_DV
N_DIL_HEADS = len(DIL_CONFIGS) * DIL_HEADS_PER_GROUP
DIL_W = N_DIL_HEADS * DIL_HD
DIL_OUT = DIL_HEADS_PER_GROUP * DIL_HD
MEM_Q = MEM_HEADS * MEM_HD
D_IN = 2 * DIFF_QK + DIFF_V + 3 * DIL_W + MEM_Q
SRC_VD = 2 * DIFF_QK
SRC_QL = SRC_VD + DIFF_V
SRC_KL = SRC_QL + DIL_W
SRC_VL = SRC_KL + DIL_W
SRC_QM = SRC_VL + DIL_W
D_MAIN = 2 * DIFF_QK + MEM_Q
OFF_QD, OFF_KD, OFF_QM = 0, DIFF_QK, 2 * DIFF_QK
D_DIL = 3 * DIL_OUT
SEQ_TILE = 512
POS_SPLIT = 64
ONES_ROWS = 16
SKIP_LOG2 = 140.0
NORM_MARGIN = 1.02

VMEM_LIMIT = 56 * 1024 * 1024
EXPERT_ROWS = 160
EXPERTS_PER_STEP = 4
EXPERTS_TOGETHER = 2
SPARE_ROWS = 512
ROW_GROUP = 8
ROW_TILE_WORDS = 8 * 128
MOE_TILE = 4096


def _cparams(sem):
    return pltpu.CompilerParams(dimension_semantics=sem, vmem_limit_bytes=VMEM_LIMIT)


def _ln(x, g, b):
    mu = jnp.mean(x, axis=-1, keepdims=True)
    xc = x - mu
    var = jnp.mean(xc * xc, axis=-1, keepdims=True)
    return xc * lax.rsqrt(var + LN_EPS) * g + b


def _alibi_slopes(n):
    return np.asarray(2.0 ** (-8.0 * np.arange(1, n + 1) / n), dtype=np.float32)


def _ln_proj_kernel(x_ref, g_ref, b_ref, w_ref, wvt_ref, grp_ref, main_ref, d0_ref, d1_ref, d2_ref, vt_ref, nrm_ref,
                    xn_ref):
    xn_ref[...] = _ln(x_ref[...], g_ref[...], b_ref[...]).astype(BF16)
    col = 0
    for o_ref in (main_ref, d0_ref, d1_ref, d2_ref):
        for n in range(o_ref.shape[1] // D_DIL):
            o_ref[:, n * D_DIL:(n + 1) * D_DIL] = jnp.dot(
                xn_ref[...], w_ref[:, col:col + D_DIL], preferred_element_type=F32).astype(o_ref.dtype)
            col += D_DIL
    vt_ref[0] = lax.dot_general(wvt_ref[...], xn_ref[...], (((1,), (1,)), ((), ())),
                                preferred_element_type=F32).astype(vt_ref.dtype)
    qk = main_ref[:, :2 * DIFF_QK].astype(F32)
    sq = jnp.dot((qk * qk).astype(BF16), grp_ref[...], preferred_element_type=F32)
    nrm_ref[0] = jnp.max(sq, axis=0, keepdims=True)


def _ln_proj(x, g, b, w, wvt, tm=SEQ_TILE):
    T, D = x.shape
    N = w.shape[1]
    NV = wvt.shape[0]
    assert N == D_MAIN + 3 * D_DIL and D_MAIN % D_DIL == 0
    tok = lambda width: pl.BlockSpec((tm, width), lambda i: (i, 0))
    cols = lax.broadcasted_iota(jnp.int32, (2 * DIFF_QK, 128), 0)
    grp = (cols // DIFF_DK == lax.broadcasted_iota(jnp.int32, (2 * DIFF_QK, 128), 1)).astype(BF16)
    return pl.pallas_call(
        _ln_proj_kernel,
        name="ln_proj",
        grid=(T // tm,),
        in_specs=[pl.BlockSpec((tm, D), lambda i: (i, 0)),
                  pl.BlockSpec((1, D), lambda i: (0, 0)),
                  pl.BlockSpec((1, D), lambda i: (0, 0)),
                  pl.BlockSpec((D, N), lambda i: (0, 0)),
                  pl.BlockSpec((NV, D), lambda i: (0, 0)),
                  pl.BlockSpec((2 * DIFF_QK, 128), lambda i: (0, 0))],
        out_specs=[tok(D_MAIN), tok(D_DIL), tok(D_DIL), tok(D_DIL),
                   pl.BlockSpec((1, NV, tm), lambda i: (i, 0, 0)),
                   pl.BlockSpec((1, 1, 128), lambda i: (i, 0, 0))],
        out_shape=[jax.ShapeDtypeStruct((T, D_MAIN), BF16)] + [jax.ShapeDtypeStruct((T, D_DIL), BF16)] * 3
        + [jax.ShapeDtypeStruct((T // tm, NV, tm), BF16), jax.ShapeDtypeStruct((T // tm, 1, 128), F32)],
        scratch_shapes=[pltpu.VMEM((tm, D), BF16)],
        compiler_params=_cparams(("parallel",)),
    )(x, g, b, w, wvt, grp)


def _split3_bf16(x):
    hi = x.astype(BF16).astype(F32)
    mid = (x - hi).astype(BF16).astype(F32)
    return hi, mid, x - hi - mid


def _diff_attn_kernel(lo_ref, hi_ref, slope_ref, lam_ref, q_ref, k_ref, kpos_ref, vt_ref, g_ref, o_ref,
                      qx_ref, m_ref, acc_ref, sa_ref, sb_ref, *, tq, tk, lambda_init):
    h = pl.program_id(1)
    qi = pl.program_id(2)
    slope = slope_ref[h]
    q = q_ref[0]
    lane = lax.broadcasted_iota(jnp.int32, q.shape, 1)
    ipos = (qi * tq + lax.broadcasted_iota(jnp.int32, q.shape, 0)).astype(F32)
    a_hi, a_mid, a_lo = _split3_bf16(-slope * ipos)
    c = jnp.full(q.shape, slope, F32)
    c_hi = c.astype(BF16).astype(F32)
    c_lo = c - c_hi
    feats = (a_hi, a_mid, a_lo, POS_SPLIT * c_hi, c_hi, POS_SPLIT * c_lo, c_lo)
    ext = jnp.zeros(q.shape, F32)
    for n, f in enumerate(feats):
        ext = jnp.where(lane == n, f, ext)
    for comp in range(2):
        qz = jnp.where((lane >= comp * DIFF_DK) & (lane < (comp + 1) * DIFF_DK), q, jnp.zeros_like(q))
        qx_ref[comp, 0] = jnp.concatenate([qz, ext.astype(BF16)], axis=1)
        qx_ref[comp, 1] = jnp.concatenate([qz, (-ext).astype(BF16)], axis=1)
        qx_ref[comp, 2] = jnp.concatenate([qz, jnp.zeros_like(qz)], axis=1)
    m_ref[...] = jnp.full(m_ref.shape, NEG_INF, F32)
    acc_ref[...] = jnp.zeros(acc_ref.shape, F32)
    ones = jnp.where(lax.broadcasted_iota(jnp.int32, (ONES_ROWS, tk), 0) == 0, 1.0, 0.0).astype(BF16)

    def scores(j, side, comp):
        kx = jnp.concatenate([k_ref[0, j], kpos_ref[j]], axis=1)
        return lax.dot_general(kx, qx_ref[comp, side], (((1,), (1,)), ((), ())), preferred_element_type=F32)

    def softmax_pv(st, j, comp):
        vx = jnp.concatenate([vt_ref[0, j], ones], axis=0)
        m_old = m_ref[comp]
        m_new = jnp.maximum(m_old, jnp.max(st, axis=0, keepdims=True))
        pt = jnp.exp2(st - m_new).astype(BF16)
        acc_ref[comp] = jnp.exp2(m_old - m_new) * acc_ref[comp] + jnp.dot(vx, pt, preferred_element_type=F32)
        m_ref[comp] = m_new

    step = (pl.program_id(0) * pl.num_programs(1) + h) * pl.num_programs(2) + qi
    lo = lo_ref[step]
    n_pos = hi_ref[step] - lo + 1

    def key_block(b):
        j = lo + b - 1
        side = (j >= qi).astype(jnp.int32)
        return jnp.where(b == 0, qi, j + side), side

    def fill(b, dst_ref):
        j, side = key_block(b)
        for comp in range(2):
            dst_ref[comp] = scores(j, side, comp)

    def drain(b, src_ref):
        j, _ = key_block(b)
        for comp in range(2):
            softmax_pv(src_ref[comp], j, comp)

    def trip(t, carry):
        fill(2 * t + 1, sb_ref)
        drain(2 * t, sa_ref)
        fill(2 * t + 2, sa_ref)
        drain(2 * t + 1, sb_ref)
        return carry

    jpos = qi * tk + lax.broadcasted_iota(jnp.int32, (tk, tq), 0)
    ipos_t = qi * tq + lax.broadcasted_iota(jnp.int32, (tk, tq), 1)
    bias = slope * jnp.abs(jpos - ipos_t).astype(F32)
    for comp in range(2):
        sa_ref[comp] = scores(qi, 2, comp) - bias
    lax.fori_loop(0, (n_pos - 1) // 2, trip, 0)

    @pl.when(n_pos % 2 == 0)
    def _():
        fill(n_pos - 1, sb_ref)
        drain(n_pos - 2, sa_ref)
        drain(n_pos - 1, sb_ref)

    @pl.when(n_pos % 2 == 1)
    def _():
        drain(n_pos - 1, sa_ref)

    o0 = acc_ref[0, :DIFF_DV] / acc_ref[0, DIFF_DV:DIFF_DV + 1]
    o1 = acc_ref[1, :DIFF_DV] / acc_ref[1, DIFF_DV:DIFF_DV + 1]
    o = o0 - lam_ref[0] * o1
    ms = jnp.mean(o * o, axis=0, keepdims=True)
    o = o * lax.rsqrt(ms + LN_EPS) * g_ref[...] * (1.0 - lambda_init)
    o_ref[0] = o.T.astype(o_ref.dtype)


def _active_key_range(norms, slopes, tile):
    B, nb, _ = norms.shape
    nh = DIFF_HEADS * 2
    qn = (jnp.sqrt(norms[:, :, :nh]) * NORM_MARGIN).reshape(B, nb, DIFF_HEADS, 2)
    kn = (jnp.sqrt(norms[:, :, nh:2 * nh]) * NORM_MARGIN).reshape(B, nb, DIFF_HEADS, 2)
    blk = jnp.arange(nb, dtype=jnp.int32)
    gap = jnp.abs(blk[:, None] - blk[None, :])
    dmin = jnp.maximum((gap - 1) * tile + 1, 0).astype(F32)
    bound = (qn[:, :, None] * (kn[:, None, :] + kn[:, :, None])
             - (slopes[None, None, None, :, None] * dmin[None, :, :, None, None]))
    active = jnp.any(bound > -SKIP_LOG2, axis=-1) | (gap == 0)[None, :, :, None]
    lo = jnp.min(jnp.where(active, blk[None, None, :, None], nb), axis=2)
    hi = jnp.max(jnp.where(active, blk[None, None, :, None], -1), axis=2)
    to_bhi = lambda a: a.transpose(0, 2, 1).reshape(-1).astype(jnp.int32)
    return to_bhi(lo), to_bhi(hi)


def _diff_attn(proj, vt, norms, slopes, lam, g_col, lambda_init):
    B, S, _ = proj.shape
    tq = tk = min(SEQ_TILE, S)
    nk = S // tk
    pos = lax.broadcasted_iota(jnp.int32, (nk, tk, 128), 0) * tk + lax.broadcasted_iota(jnp.int32, (nk, tk, 128), 1)
    lane = lax.broadcasted_iota(jnp.int32, (nk, tk, 128), 2)
    kpos = jnp.where(lane < 3, 1, jnp.where((lane == 3) | (lane == 5), pos // POS_SPLIT,
                                            jnp.where((lane == 4) | (lane == 6), pos % POS_SPLIT, 0))).astype(BF16)
    lo, hi = _active_key_range(norms.reshape(B, nk, 128), slopes, tk)
    qb, kb = OFF_QD // 128, OFF_KD // 128
    grid_spec = pltpu.PrefetchScalarGridSpec(
        num_scalar_prefetch=2,
        grid=(B, DIFF_HEADS, S // tq),
        in_specs=[pl.BlockSpec(memory_space=pltpu.SMEM),
                  pl.BlockSpec(memory_space=pltpu.SMEM),
                  pl.BlockSpec((1, tq, 128), lambda b, h, i, lo, hi: (b, i, qb + h)),
                  pl.BlockSpec((1, nk, tk, 128), lambda b, h, i, lo, hi: (b, 0, 0, kb + h)),
                  pl.BlockSpec((nk, tk, 128), lambda b, h, i, lo, hi: (0, 0, 0)),
                  pl.BlockSpec((1, nk, DIFF_DV, tk), lambda b, h, i, lo, hi: (b, 0, h, 0)),
                  pl.BlockSpec((DIFF_DV, 1), lambda b, h, i, lo, hi: (0, 0))],
        out_specs=pl.BlockSpec((1, tq, 128), lambda b, h, i, lo, hi: (b, i, h)),
        scratch_shapes=[pltpu.VMEM((2, 3, tq, 256), BF16), pltpu.VMEM((2, 1, tq), F32),
                        pltpu.VMEM((2, DIFF_DV + ONES_ROWS, tq), F32),
                        pltpu.VMEM((2, tk, tq), F32), pltpu.VMEM((2, tk, tq), F32)],
    )
    return pl.pallas_call(
        functools.partial(_diff_attn_kernel, tq=tq, tk=tk, lambda_init=lambda_init),
        name="diff_attn",
        grid_spec=grid_spec,
        out_shape=jax.ShapeDtypeStruct((B, S, DIFF_V), BF16),
        compiler_params=_cparams(("parallel", "parallel", "parallel")),
    )(lo, hi, slopes, lam, proj, proj.reshape(B, nk, tk, -1), kpos, vt.reshape(B, nk, DIFF_V, tk), g_col)


def _dilated_kernel(slope_ref, q_ref, *refs, tl, half, n_kb, length, group, dilation):
    k_refs, v_refs, (o_ref, lse_ref) = refs[:n_kb], refs[n_kb:2 * n_kb], refs[2 * n_kb:]
    i = pl.program_id(2)
    q = q_ref[0]
    kcat = jnp.concatenate([r[0] for r in k_refs], axis=0)
    vcat = jnp.concatenate([r[0] for r in v_refs], axis=0)
    nkeys = n_kb * half
    nh = DIL_HEADS_PER_GROUP
    lane = lax.broadcasted_iota(jnp.int32, q.shape, 1)
    qstack = jnp.concatenate(
        [jnp.where((lane >= h * DIL_HD) & (lane < (h + 1) * DIL_HD), q, jnp.zeros_like(q)) for h in range(nh)], axis=0)
    row = lax.broadcasted_iota(jnp.int32, (nh * tl, 1), 0)
    slope = jnp.zeros((nh * tl, 1), F32)
    for h in range(nh):
        slope = jnp.where(row // tl == h, slope_ref[group * nh + h] * dilation, slope)
    qpos = i * tl + lax.broadcasted_iota(jnp.int32, (nh * tl, nkeys), 0) % tl
    kpos = i * tl - half + lax.broadcasted_iota(jnp.int32, (nh * tl, nkeys), 1)
    rel = jnp.abs(qpos - kpos)
    ok = (rel <= half) & (kpos >= 0) & (kpos < length)
    s = lax.dot_general(qstack, kcat, (((1,), (1,)), ((), ())), preferred_element_type=F32)
    s = jnp.where(ok, s - slope * rel.astype(F32), NEG_INF)
    m = jnp.max(s, axis=-1, keepdims=True)
    p = jnp.exp(s - m)
    l = jnp.sum(p, axis=-1, keepdims=True)
    pv = jnp.dot(p.astype(BF16), vcat, preferred_element_type=F32) / l
    lse = m + jnp.log(l)
    olane = lax.broadcasted_iota(jnp.int32, (tl, DIL_OUT), 1)
    o_acc = jnp.zeros((tl, DIL_OUT), F32)
    lse_acc = jnp.zeros((tl, DIL_OUT), F32)
    for h in range(nh):
        osel = (olane >= h * DIL_HD) & (olane < (h + 1) * DIL_HD)
        o_acc = jnp.where(osel, pv[h * tl:(h + 1) * tl], o_acc)
        lse_acc = jnp.where(osel, lse[h * tl:(h + 1) * tl], lse_acc)
    o_ref[0] = o_acc.astype(o_ref.dtype)
    lse_ref[0] = lse_acc


def _dilated_group(qkv, slopes, group, tl=128):
    B, S, _ = qkv.shape
    window, dil = DIL_CONFIGS[group]
    L = S // dil
    half = window // (2 * dil)
    tl = min(tl, L)
    assert tl % half == 0 and L % tl == 0
    per = tl // half
    n_kb = per + 2
    last_kb = L // half - 1
    pv = qkv.reshape(B, L, dil * D_DIL)
    nc = D_DIL // DIL_OUT
    qc, kc, vc = 0, 1, 2

    def kv_spec(col, n):
        def imap(b, r, i):
            return (b, jnp.clip(i * per - 1 + n, 0, last_kb), r * nc + col)
        return pl.BlockSpec((1, half, DIL_OUT), imap)

    q_spec = pl.BlockSpec((1, tl, DIL_OUT), lambda b, r, i: (b, i, r * nc + qc))
    out_spec = pl.BlockSpec((1, tl, DIL_OUT), lambda b, r, i: (b, i, r))
    o, lse = pl.pallas_call(
        functools.partial(_dilated_kernel, tl=tl, half=half, n_kb=n_kb, length=L, group=group, dilation=float(dil)),
        name=f"dilated_attn_{group}",
        grid=(B, dil, L // tl),
        in_specs=[pl.BlockSpec(memory_space=pltpu.SMEM), q_spec]
        + [kv_spec(kc, n) for n in range(n_kb)] + [kv_spec(vc, n) for n in range(n_kb)],
        out_specs=[out_spec, out_spec],
        out_shape=[jax.ShapeDtypeStruct((B, L, dil * DIL_OUT), BF16),
                   jax.ShapeDtypeStruct((B, L, dil * DIL_OUT), F32)],
        compiler_params=_cparams(("parallel", "parallel", "parallel")),
    )(slopes, *([pv] * (1 + 2 * n_kb)))
    return o.reshape(B, S, DIL_OUT), lse.reshape(B, S, DIL_OUT)


def _mem_kv_kernel(m_ref, g_ref, b_ref, w_ref, o_ref):
    mn = _ln(m_ref[...], g_ref[...], b_ref[...]).astype(BF16)
    o_ref[...] = jnp.dot(mn, w_ref[...], preferred_element_type=F32).astype(o_ref.dtype)


def _mem_kv(mem2d, g, b, w, tm=256):
    T, D = mem2d.shape
    N = w.shape[1]
    return pl.pallas_call(
        _mem_kv_kernel,
        name="mem_kv",
        grid=(T // tm,),
        in_specs=[pl.BlockSpec((tm, D), lambda i: (i, 0)),
                  pl.BlockSpec((1, D), lambda i: (0, 0)),
                  pl.BlockSpec((1, D), lambda i: (0, 0)),
                  pl.BlockSpec((D, N), lambda i: (0, 0))],
        out_specs=pl.BlockSpec((tm, N), lambda i: (i, 0)),
        out_shape=jax.ShapeDtypeStruct((T, N), BF16),
        compiler_params=_cparams(("parallel",)),
    )(mem2d, g, b, w)


def _mem_attn_kernel(q_ref, kv_ref, o_ref):
    q = q_ref[0]
    k = kv_ref[0, :, :MEM_Q]
    v = kv_ref[0, :, MEM_Q:]
    lane = lax.broadcasted_iota(jnp.int32, q.shape, 1)
    o_acc = jnp.zeros(q.shape, F32)
    for h in range(MEM_HEADS):
        sel = (lane >= h * MEM_HD) & (lane < (h + 1) * MEM_HD)
        qz = jnp.where(sel, q, jnp.zeros_like(q))
        s = lax.dot_general(qz, k, (((1,), (1,)), ((), ())), preferred_element_type=F32)
        m = jnp.max(s, axis=-1, keepdims=True)
        p = jnp.exp(s - m)
        l = jnp.sum(p, axis=-1, keepdims=True)
        pv = jnp.dot(p.astype(BF16), v, preferred_element_type=F32) / l
        o_acc = jnp.where(sel, pv, o_acc)
    o_ref[0] = o_acc.astype(o_ref.dtype)


def _mem_attn(proj, kv, tm=512):
    B, S, _ = proj.shape
    M = kv.shape[1]
    tm = min(tm, S)
    qc = OFF_QM // MEM_Q
    return pl.pallas_call(
        _mem_attn_kernel,
        name="mem_attn",
        grid=(B, S // tm),
        in_specs=[pl.BlockSpec((1, tm, MEM_Q), lambda b, i: (b, i, qc)),
                  pl.BlockSpec((1, M, 2 * MEM_Q), lambda b, i: (b, 0, 0))],
        out_specs=pl.BlockSpec((1, tm, MEM_Q), lambda b, i: (b, i, 0)),
        out_shape=jax.ShapeDtypeStruct((B, S, MEM_Q), BF16),
        compiler_params=_cparams(("parallel", "parallel")),
    )(proj, kv)


HI16 = -65536


def _pack_halves(x):
    half = x.shape[1] // 2
    lo = lax.bitcast_convert_type(x[:, :half].astype(BF16).astype(F32), jnp.int32)
    hi = lax.bitcast_convert_type(x[:, half:].astype(BF16).astype(F32), jnp.int32)
    return lax.shift_right_logical(lo, 16) | (hi & HI16)


def _unpack_halves(r):
    lo = lax.bitcast_convert_type(lax.shift_left(r, 16), F32).astype(BF16)
    hi = lax.bitcast_convert_type(r & HI16, F32).astype(BF16)
    return lo, hi


def _merge_kernel(x_ref, od_ref, o0_ref, o1_ref, o2_ref, l0_ref, l1_ref, l2_ref, om_ref,
                  lng_ref, lnb_ref, wg_ref, bg_ref, wbd_ref, wbl_ref, wbm_ref, wo_ref, g1_ref, b1_ref,
                  x1_ref):
    D = x_ref.shape[1]
    xn = _ln(x_ref[...], lng_ref[...], lnb_ref[...])
    xb = xn.astype(BF16)
    l0, l1, l2 = l0_ref[...], l1_ref[...], l2_ref[...]
    mx = jnp.maximum(jnp.maximum(l0, l1), l2)
    e0, e1, e2 = jnp.exp(l0 - mx), jnp.exp(l1 - mx), jnp.exp(l2 - mx)
    ol = (e0 * o0_ref[...].astype(F32) + e1 * o1_ref[...].astype(F32) + e2 * o2_ref[...].astype(F32)) / (e0 + e1 + e2)
    branches = (jnp.dot(od_ref[...], wbd_ref[...], preferred_element_type=F32),
                jnp.dot(ol.astype(BF16), wbl_ref[...], preferred_element_type=F32),
                jnp.dot(om_ref[...], wbm_ref[...], preferred_element_type=F32))
    merged = jnp.zeros(xn.shape, F32)
    for n, br in enumerate(branches):
        cols = slice(n * D, (n + 1) * D)
        gate = jax.nn.sigmoid(jnp.dot(xb, wg_ref[:, cols], preferred_element_type=F32) + bg_ref[:, cols])
        merged = merged + gate * br
    y = DEEPNORM_ALPHA * xn + jnp.dot(merged.astype(BF16), wo_ref[...], preferred_element_type=F32)
    x1_ref[...] = _ln(y, g1_ref[...], b1_ref[...])


def _merge(x, od, odil, lses, om, lng, lnb, wg, bg, wbd, wbl, wbm, wo, g1, b1, tm=256):
    T, D = x.shape
    row = lambda w: pl.BlockSpec((tm, w), lambda i: (i, 0))
    full = lambda a: pl.BlockSpec(a.shape, lambda i: (0,) * a.ndim)
    params = (lng, lnb, wg, bg, wbd, wbl, wbm, wo, g1, b1)
    return pl.pallas_call(
        _merge_kernel,
        name="merge_out",
        grid=(T // tm,),
        in_specs=[row(D), row(D)] + [row(DIL_OUT)] * 7 + [full(a) for a in params],
        out_specs=row(D),
        out_shape=jax.ShapeDtypeStruct((T, D), F32),
        compiler_params=_cparams(("parallel",)),
    )(x, od, *odil, *lses, om, *params)


def _router_kernel(x_ref, w_ref, bias_ref, idx_ref, xr_ref, cnt_ref):
    x = x_ref[...]
    logits = jnp.dot(x.astype(BF16), w_ref[...], preferred_element_type=F32)
    scores = jax.nn.sigmoid(logits)
    sel = scores + bias_ref[...]
    tm, E = sel.shape
    lane = lax.broadcasted_iota(jnp.int32, (tm, E), 1)
    olane = lax.broadcasted_iota(jnp.int32, idx_ref.shape, 1)
    idx_out = jnp.zeros(idx_ref.shape, jnp.int32)
    chosen = jnp.zeros((tm, E), jnp.bool_)
    for k in range(TOP_K):
        mx = jnp.max(sel, axis=-1, keepdims=True)
        idx = jnp.min(jnp.where(sel == mx, lane, E), axis=-1, keepdims=True)
        hit = lane == idx
        chosen = chosen | hit
        sel = jnp.where(hit, -jnp.inf, sel)
        idx_out = jnp.where(olane == k, idx, idx_out)
    picked = jnp.where(chosen, scores, 0.0)
    idx_ref[...] = idx_out
    cnt_ref[0] = jnp.sum(chosen.astype(jnp.int32), axis=0, keepdims=True)
    rw = picked / jnp.sum(picked, axis=-1, keepdims=True) * ROUTED_SCALE
    words = jnp.concatenate([_pack_halves(x), lax.bitcast_convert_type(rw, jnp.int32)], axis=1)
    pad = ROW_TILE_WORDS - words.shape[1]
    if pad:
        words = jnp.concatenate([words, jnp.zeros((tm, pad), jnp.int32)], axis=1)
    for s in range(8):
        xr_ref[pl.ds(s, tm, stride=8), :] = words[:, s * 128:(s + 1) * 128]


def _router(x1, w, bias, tm=512):
    T, D = x1.shape
    E = w.shape[1]
    return pl.pallas_call(
        _router_kernel,
        name="router",
        grid=(T // tm,),
        in_specs=[pl.BlockSpec((tm, D), lambda i: (i, 0)),
                  pl.BlockSpec((D, E), lambda i: (0, 0)),
                  pl.BlockSpec((1, E), lambda i: (0, 0))],
        out_specs=[pl.BlockSpec((tm, 128), lambda i: (i, 0)), pl.BlockSpec((tm * 8, 128), lambda i: (i, 0)),
                   pl.BlockSpec((1, 1, E), lambda i: (i, 0, 0))],
        out_shape=[jax.ShapeDtypeStruct((T, 128), jnp.int32), jax.ShapeDtypeStruct((T * 8, 128), jnp.int32),
                   jax.ShapeDtypeStruct((T // tm, 1, E), jnp.int32)],
        compiler_params=_cparams(("parallel",)),
    )(x1, w, bias)


def _moe_kernel(offs_ref, tok_ref, xr_ref, wg_ref, wu_ref, wd_ref, out_ref, rows_ref, y_ref, *, chunk, n_experts):
    tile = pl.program_id(0)
    step = pl.program_id(1)

    @pl.when(step == 0)
    def _():
        out_ref[...] = jnp.zeros(out_ref.shape, F32)

    @pl.when((tile == 0) & (step == 0))
    def _():
        rows_ref[...] = jnp.zeros(rows_ref.shape, rows_ref.dtype)

    half = wg_ref.shape[1] // 2
    n_together = rows_ref.shape[0]
    lax.fori_loop(0, wg_ref.shape[0] // n_together, functools.partial(
        _moe_expert_group, offs_ref=offs_ref, tok_ref=tok_ref, xr_ref=xr_ref, wg_ref=wg_ref, wu_ref=wu_ref,
        wd_ref=wd_ref, out_ref=out_ref, rows_ref=rows_ref, y_ref=y_ref, chunk=chunk, n_experts=n_experts,
        half=half, tile=tile, first=step * wg_ref.shape[0]), 0)


def _moe_expert_group(gg, carry, *, offs_ref, tok_ref, xr_ref, wg_ref, wu_ref, wd_ref, out_ref, rows_ref, y_ref,
                      chunk, n_experts, half, tile, first):
    n_together = rows_ref.shape[0]
    ees = [gg * n_together + k for k in range(n_together)]
    experts = [first + ee for ee in ees]
    starts = [offs_ref[tile * (n_experts + 1) + e] for e in experts]
    counts = [offs_ref[tile * (n_experts + 1) + e + 1] - st for e, st in zip(experts, starts)]
    n_act = half // 128
    n_rw = -(-n_experts // 128)

    def one_chunk(c, carry):
        cnts = [jnp.clip(n - c * chunk, 0, chunk) for n in counts]
        bases = [jnp.where(cnt > 0, st + c * chunk, 0) for st, cnt in zip(starts, cnts)]

        for k in range(n_together):
            for i in range(chunk):
                rows_ref[k, i * 8:(i + 1) * 8, :] = xr_ref[tok_ref[0, 0, bases[k] + i]]

        for k in range(n_together):
            ee = ees[k]

            def slab(s):
                return rows_ref[k, pl.ds(s, chunk, stride=8), :]

            lo, hi = _unpack_halves(jnp.concatenate([slab(s) for s in range(n_act)], axis=1))
            g = (jnp.dot(lo, wg_ref[ee, :half], preferred_element_type=F32)
                 + jnp.dot(hi, wg_ref[ee, half:], preferred_element_type=F32))
            u = (jnp.dot(lo, wu_ref[ee, :half], preferred_element_type=F32)
                 + jnp.dot(hi, wu_ref[ee, half:], preferred_element_type=F32))
            wrow = lax.bitcast_convert_type(
                jnp.concatenate([slab(n_act + s) for s in range(n_rw)], axis=1)[:, :n_experts], F32)
            lane = lax.broadcasted_iota(jnp.int32, wrow.shape, 1)
            wcol = jnp.sum(jnp.where(lane == experts[k], wrow, 0.0), axis=1, keepdims=True)
            wcol = jnp.where(lax.broadcasted_iota(jnp.int32, wcol.shape, 0) < cnts[k], wcol, 0.0)
            hid = (g * jax.nn.sigmoid(g)) * u * wcol
            y = jnp.dot(hid.astype(BF16), wd_ref[ee], preferred_element_type=F32)
            for s in range(y.shape[1] // 128):
                y_ref[k, pl.ds(s, chunk, stride=8), :] = y[:, s * 128:(s + 1) * 128]

        n_tok = out_ref.shape[0] - SPARE_ROWS
        for k in range(n_together):
            for g in range(chunk // ROW_GROUP):
                rows = []
                for s in range(ROW_GROUP):
                    i = g * ROW_GROUP + s
                    rows.append(jnp.where(i < cnts[k], tok_ref[0, 0, bases[k] + i], n_tok + s))
                olds = [out_ref[t] for t in rows]
                for s in range(ROW_GROUP):
                    i = g * ROW_GROUP + s
                    out_ref[rows[s]] = olds[s] + y_ref[k, i * 8:(i + 1) * 8, :]
        return carry

    n_max = functools.reduce(jnp.maximum, counts)
    lax.fori_loop(0, (n_max + chunk - 1) // chunk, one_chunk, 0)
    return carry


def _moe_experts(offs, tok, xr, wg, wu, wd, tile):
    T = xr.shape[0]
    E, D, F = wg.shape
    assert D == ROW_TILE_WORDS and D // 2 + E <= ROW_TILE_WORDS and (D // 2) % 128 == 0
    nt = T // tile
    eps = math.gcd(E, EXPERTS_PER_STEP)
    grid_spec = pltpu.PrefetchScalarGridSpec(
        num_scalar_prefetch=1,
        grid=(nt, E // eps),
        in_specs=[pl.BlockSpec((1, 1, tok.shape[2]), lambda t, e, offs: (t, 0, 0), memory_space=pltpu.SMEM),
                  pl.BlockSpec((tile, 8, 128), lambda t, e, offs: (t, 0, 0), pipeline_mode=pl.Buffered(1)),
                  pl.BlockSpec((eps, D, F), lambda t, e, offs: (e, 0, 0)),
                  pl.BlockSpec((eps, D, F), lambda t, e, offs: (e, 0, 0)),
                  pl.BlockSpec((eps, F, D), lambda t, e, offs: (e, 0, 0))],
        out_specs=pl.BlockSpec((tile + SPARE_ROWS, 8, 128), lambda t, e, offs: (t, 0, 0),
                               pipeline_mode=pl.Buffered(1)),
        scratch_shapes=[pltpu.VMEM((EXPERTS_TOGETHER, EXPERT_ROWS * 8, 128), jnp.int32),
                        pltpu.VMEM((EXPERTS_TOGETHER, EXPERT_ROWS * 8, 128), F32)],
    )
    return pl.pallas_call(
        functools.partial(_moe_kernel, chunk=EXPERT_ROWS, n_experts=E),
        name="moe_experts",
        grid_spec=grid_spec,
        out_shape=jax.ShapeDtypeStruct((nt * (tile + SPARE_ROWS), 8, 128), F32),
        compiler_params=_cparams(("arbitrary", "arbitrary")),
    )(offs, tok, xr, wg, wu, wd)


def _final_kernel(x1_ref, r_ref, wg_ref, wu_ref, wd_ref, g_ref, b_ref, o_ref):
    xb = x1_ref[...].astype(BF16)
    g = jnp.dot(xb, wg_ref[...], preferred_element_type=F32)
    u = jnp.dot(xb, wu_ref[...], preferred_element_type=F32)
    hid = (g * jax.nn.sigmoid(g)) * u
    shared = jnp.dot(hid.astype(BF16), wd_ref[...], preferred_element_type=F32)
    tm = x1_ref.shape[0]
    routed = jnp.concatenate([r_ref[pl.ds(s, tm, stride=8), :] for s in range(8)], axis=1)
    y = DEEPNORM_ALPHA * x1_ref[...] + (routed + shared)
    o_ref[...] = _ln(y, g_ref[...], b_ref[...])


def _final(x1, routed, moe_tile, wg, wu, wd, g, b, tm=SPARE_ROWS):
    T, D = x1.shape
    per = moe_tile // tm
    row = pl.BlockSpec((tm, D), lambda i: (i, 0))
    full = lambda a: pl.BlockSpec(a.shape, lambda i: (0,) * a.ndim)
    params = (wg, wu, wd, g, b)
    return pl.pallas_call(
        _final_kernel,
        name="shared_final",
        grid=(T // tm,),
        in_specs=[row, pl.BlockSpec((tm * 8, 128), lambda i: (i // per * (per + 1) + i % per, 0))]
        + [full(a) for a in params],
        out_specs=row,
        out_shape=jax.ShapeDtypeStruct((T, D), F32),
        compiler_params=_cparams(("parallel",)),
    )(x1, routed.reshape(-1, 128), *params)


def _moe_routed(xr, top_idx, counts, wg, wu, wd):
    T = xr.shape[0]
    E = wg.shape[0]
    tile = min(MOE_TILE, T)
    nt = T // tile
    n_assign = tile * TOP_K
    pos = lax.broadcasted_iota(jnp.int32, (nt, n_assign), 1)
    order = jnp.sort(top_idx.reshape(nt, n_assign) * n_assign + pos, axis=1) % n_assign
    tok = jnp.pad(order // TOP_K, ((0, 0), (0, EXPERT_ROWS))).reshape(nt, 1, -1)
    per_tile = counts.reshape(nt, -1, E).sum(axis=1)
    offs = jnp.concatenate([jnp.zeros((nt, 1), jnp.int32), jnp.cumsum(per_tile, axis=1, dtype=jnp.int32)], axis=1)
    return _moe_experts(offs.reshape(-1), tok, xr, wg, wu, wd, tile), tile


def _trunk(x, mem, p):
    B, S, D = x.shape
    T = B * S
    x2 = x.reshape(T, D)
    proj, d0, d1, d2, vt, norms = _ln_proj(x2, p['ln_in_g'], p['ln_in_b'], p['w_in'], p['w_vt'])
    proj = proj.reshape(B, S, D_MAIN)
    od = _diff_attn(proj, vt, norms, p['diff_slopes'], p['lam'], p['diff_subln_g'], p['lambda_init'])
    dil = [_dilated_group(d.reshape(B, S, D_DIL), p['dil_slopes'], g) for g, d in enumerate((d0, d1, d2))]
    kv = _mem_kv(mem.reshape(-1, D), p['ln_mem_g'], p['ln_mem_b'], p['w_mem_kv']).reshape(B, mem.shape[1], 2 * MEM_Q)
    om = _mem_attn(proj, kv)
    x1 = _merge(x2, od.reshape(T, DIFF_V), [o.reshape(T, DIL_OUT) for o, _ in dil],
                [l.reshape(T, DIL_OUT) for _, l in dil], om.reshape(T, MEM_Q),
                p['ln_in_g'], p['ln_in_b'], p['w_gate'], p['b_gate'], p['w_br_diff'], p['w_br_dil'],
                p['w_br_mem'], p['w_out'], p['ln1_g'], p['ln1_b'])
    idx, xr, counts = _router(x1, p['w_router'], p['router_bias'])
    routed, moe_tile = _moe_routed(xr.reshape(T, 8, 128), idx[:, :TOP_K], counts,
                                   p['w_e_gate'], p['w_e_up'], p['w_e_down'])
    y = _final(x1, routed, moe_tile, p['w_s_gate'], p['w_s_up'], p['w_s_down'], p['ln2_g'], p['ln2_b'])
    return y.reshape(B, S, D)


def kernel(x_prompt, x_sample, mem_prompt, mem_sample, ln_in_g, ln_in_b, ln_mem_g, ln_mem_b, w_in, w_gate, b_gate,
           lambda_q1, lambda_k1, lambda_q2, lambda_k2, diff_subln_g, w_br_diff, w_br_dil, w_br_mem, w_mem_kv,
           w_out, ln1_g, ln1_b, w_router, router_bias, w_e_gate, w_e_up, w_e_down, w_s_gate, w_s_up, w_s_down,
           ln2_g, ln2_b):
    l = 0
    lambda_init = 0.8 - 0.6 * math.exp(-0.3 * l)
    lam = (jnp.exp(jnp.sum(lambda_q1[l] * lambda_k1[l])) - jnp.exp(jnp.sum(lambda_q2[l] * lambda_k2[l]))
           + lambda_init).astype(F32).reshape(1)
    colscale = np.ones((D_IN,), np.float32)
    for off, width, hd in ((0, DIFF_QK, DIFF_DK), (SRC_QL, DIL_W, DIL_HD), (SRC_QM, MEM_Q, MEM_HD)):
        colscale[off:off + width] = hd ** -0.5
    colscale[:DIFF_QK] *= LOG2E
    w_all = (w_in[l] * colscale).astype(BF16)
    cols = lambda off, width: w_all[:, off:off + width]
    w_groups = [jnp.concatenate([cols(src + g * DIL_OUT, DIL_OUT) for src in (SRC_QL, SRC_KL, SRC_VL)], axis=1)
                for g in range(len(DIL_CONFIGS))]
    row = lambda a: a.reshape(1, -1).astype(F32)
    p = dict(
        ln_in_g=row(ln_in_g), ln_in_b=row(ln_in_b), ln_mem_g=row(ln_mem_g[l]), ln_mem_b=row(ln_mem_b[l]),
        w_in=jnp.concatenate([cols(0, SRC_VD), cols(SRC_QM, MEM_Q)] + w_groups, axis=1),
        w_vt=cols(SRC_VD, DIFF_V).T,
        w_gate=w_gate[l].astype(BF16), b_gate=row(b_gate[l]),
        lam=lam, lambda_init=lambda_init, diff_subln_g=diff_subln_g[l].reshape(-1, 1).astype(F32),
        diff_slopes=jnp.asarray(_alibi_slopes(DIFF_HEADS) * np.float32(LOG2E)), dil_slopes=jnp.asarray(_alibi_slopes(N_DIL_HEADS)),
        w_br_diff=w_br_diff[l].astype(BF16), w_br_dil=w_br_dil[l].astype(BF16), w_br_mem=w_br_mem[l].astype(BF16),
        w_mem_kv=w_mem_kv[l].astype(BF16), w_out=w_out[l].astype(BF16), ln1_g=row(ln1_g[l]), ln1_b=row(ln1_b[l]),
        w_router=w_router[l].astype(BF16), router_bias=row(router_bias[l]),
        w_e_gate=w_e_gate[l].astype(BF16), w_e_up=w_e_up[l].astype(BF16), w_e_down=w_e_down[l].astype(BF16),
        w_s_gate=w_s_gate[l].astype(BF16), w_s_up=w_s_up[l].astype(BF16), w_s_down=w_s_down[l].astype(BF16),
        ln2_g=row(ln2_g[l]), ln2_b=row(ln2_b[l]),
    )
    return (_trunk(x_prompt, mem_prompt, p), _trunk(x_sample, mem_sample, p))
```

```python
import functools
import math

import numpy as np
import jax
import jax.numpy as jnp
from jax import lax
from jax.experimental import pallas as pl
from jax.experimental.pallas import tpu as pltpu

F32 = jnp.float32
BF16 = jnp.bfloat16

DIFF_HEADS = 8
DIFF_DK = 64
DIFF_DV = 128
DIL_CONFIGS = ((128, 1), (512, 4), (2048, 16))
DIL_HEADS_PER_GROUP = 4
DIL_HD = 64
MEM_HEADS = 4
MEM_HD = 64
TOP_K = 8
ROUTED_SCALE = 2.5
LN_EPS = 1e-5
NEG_INF = -1e30
DEPTH = 1
DEEPNORM_ALPHA = (2 * DEPTH) ** 0.25
LOG2E = math.log2(math.e)

D_MODEL = 1024
DIFF_QK = DIFF_HEADS * 2 * DIFF_DK
DIFF_V = DIFF_HEADS * DIFF_DV
N_DIL_HEADS = len(DIL_CONFIGS) * DIL_HEADS_PER_GROUP
DIL_W = N_DIL_HEADS * DIL_HD
DIL_OUT = DIL_HEADS_PER_GROUP * DIL_HD
MEM_Q = MEM_HEADS * MEM_HD
D_IN = 2 * DIFF_QK + DIFF_V + 3 * DIL_W + MEM_Q
SRC_VD = 2 * DIFF_QK
SRC_QL = SRC_VD + DIFF_V
SRC_KL = SRC_QL + DIL_W
SRC_VL = SRC_KL + DIL_W
SRC_QM = SRC_VL + DIL_W
D_MAIN = 2 * DIFF_QK + MEM_Q
OFF_QD, OFF_KD, OFF_QM = 0, DIFF_QK, 2 * DIFF_QK
D_DIL = 3 * DIL_OUT
SEQ_TILE = 512
POS_SPLIT = 64
ONES_ROWS = 16
SKIP_LOG2 = 140.0
NORM_MARGIN = 1.02

VMEM_LIMIT = 56 * 1024 * 1024
EXPERT_ROWS = 160
EXPERTS_PER_STEP = 4
EXPERTS_TOGETHER = 2
SPARE_ROWS = 512
ROW_GROUP = 8
ROW_TILE_WORDS = 8 * 128
MOE_TILE = 4096


def _cparams(sem):
    return pltpu.CompilerParams(dimension_semantics=sem, vmem_limit_bytes=VMEM_LIMIT)


def _ln(x, g, b):
    mu = jnp.mean(x, axis=-1, keepdims=True)
    xc = x - mu
    var = jnp.mean(xc * xc, axis=-1, keepdims=True)
    return xc * lax.rsqrt(var + LN_EPS) * g + b


def _alibi_slopes(n):
    return np.asarray(2.0 ** (-8.0 * np.arange(1, n + 1) / n), dtype=np.float32)


def _ln_proj_kernel(x_ref, g_ref, b_ref, w_ref, wvt_ref, grp_ref, main_ref, d0_ref, d1_ref, d2_ref, vt_ref, nrm_ref,
                    xn_ref):
    xn_ref[...] = _ln(x_ref[...], g_ref[...], b_ref[...]).astype(BF16)
    col = 0
    for o_ref in (main_ref, d0_ref, d1_ref, d2_ref):
        for n in range(o_ref.shape[1] // D_DIL):
            o_ref[:, n * D_DIL:(n + 1) * D_DIL] = jnp.dot(
                xn_ref[...], w_ref[:, col:col + D_DIL], preferred_element_type=F32).astype(o_ref.dtype)
            col += D_DIL
    vt_ref[0] = lax.dot_general(wvt_ref[...], xn_ref[...], (((1,), (1,)), ((), ())),
                                preferred_element_type=F32).astype(vt_ref.dtype)
    qk = main_ref[:, :2 * DIFF_QK].astype(F32)
    sq = jnp.dot((qk * qk).astype(BF16), grp_ref[...], preferred_element_type=F32)
    nrm_ref[0] = jnp.max(sq, axis=0, keepdims=True)


def _ln_proj(x, g, b, w, wvt, tm=SEQ_TILE):
    T, D = x.shape
    N = w.shape[1]
    NV = wvt.shape[0]
    assert N == D_MAIN + 3 * D_DIL and D_MAIN % D_DIL == 0
    tok = lambda width: pl.BlockSpec((tm, width), lambda i: (i, 0))
    cols = lax.broadcasted_iota(jnp.int32, (2 * DIFF_QK, 128), 0)
    grp = (cols // DIFF_DK == lax.broadcasted_iota(jnp.int32, (2 * DIFF_QK, 128), 1)).astype(BF16)
    return pl.pallas_call(
        _ln_proj_kernel,
        name="ln_proj",
        grid=(T // tm,),
        in_specs=[pl.BlockSpec((tm, D), lambda i: (i, 0)),
                  pl.BlockSpec((1, D), lambda i: (0, 0)),
                  pl.BlockSpec((1, D), lambda i: (0, 0)),
                  pl.BlockSpec((D, N), lambda i: (0, 0)),
                  pl.BlockSpec((NV, D), lambda i: (0, 0)),
                  pl.BlockSpec((2 * DIFF_QK, 128), lambda i: (0, 0))],
        out_specs=[tok(D_MAIN), tok(D_DIL), tok(D_DIL), tok(D_DIL),
                   pl.BlockSpec((1, NV, tm), lambda i: (i, 0, 0)),
                   pl.BlockSpec((1, 1, 128), lambda i: (i, 0, 0))],
        out_shape=[jax.ShapeDtypeStruct((T, D_MAIN), BF16)] + [jax.ShapeDtypeStruct((T, D_DIL), BF16)] * 3
        + [jax.ShapeDtypeStruct((T // tm, NV, tm), BF16), jax.ShapeDtypeStruct((T // tm, 1, 128), F32)],
        scratch_shapes=[pltpu.VMEM((tm, D), BF16)],
        compiler_params=_cparams(("parallel",)),
    )(x, g, b, w, wvt, grp)


def _split3_bf16(x):
    hi = x.astype(BF16).astype(F32)
    mid = (x - hi).astype(BF16).astype(F32)
    return hi, mid, x - hi - mid


def _diff_attn_kernel(lo_ref, hi_ref, slope_ref, lam_ref, q_ref, k_ref, kpos_ref, vt_ref, g_ref, o_ref,
                      qx_ref, m_ref, acc_ref, sa_ref, sb_ref, *, tq, tk, lambda_init):
    h = pl.program_id(1)
    qi = pl.program_id(2)
    slope = slope_ref[h]
    q = q_ref[0]
    lane = lax.broadcasted_iota(jnp.int32, q.shape, 1)
    ipos = (qi * tq + lax.broadcasted_iota(jnp.int32, q.shape, 0)).astype(F32)
    a_hi, a_mid, a_lo = _split3_bf16(-slope * ipos)
    c = jnp.full(q.shape, slope, F32)
    c_hi = c.astype(BF16).astype(F32)
    c_lo = c - c_hi
    feats = (a_hi, a_mid, a_lo, POS_SPLIT * c_hi, c_hi, POS_SPLIT * c_lo, c_lo)
    ext = jnp.zeros(q.shape, F32)
    for n, f in enumerate(feats):
        ext = jnp.where(lane == n, f, ext)
    for comp in range(2):
        qz = jnp.where((lane >= comp * DIFF_DK) & (lane < (comp + 1) * DIFF_DK), q, jnp.zeros_like(q))
        qx_ref[comp, 0] = jnp.concatenate([qz, ext.astype(BF16)], axis=1)
        qx_ref[comp, 1] = jnp.concatenate([qz, (-ext).astype(BF16)], axis=1)
        qx_ref[comp, 2] = jnp.concatenate([qz, jnp.zeros_like(qz)], axis=1)
    m_ref[...] = jnp.full(m_ref.shape, NEG_INF, F32)
    acc_ref[...] = jnp.zeros(acc_ref.shape, F32)
    ones = jnp.where(lax.broadcasted_iota(jnp.int32, (ONES_ROWS, tk), 0) == 0, 1.0, 0.0).astype(BF16)

    def scores(j, side, comp):
        kx = jnp.concatenate([k_ref[0, j], kpos_ref[j]], axis=1)
        return lax.dot_general(kx, qx_ref[comp, side], (((1,), (1,)), ((), ())), preferred_element_type=F32)

    def softmax_pv(st, j, comp):
        vx = jnp.concatenate([vt_ref[0, j], ones], axis=0)
        m_old = m_ref[comp]
        m_new = jnp.maximum(m_old, jnp.max(st, axis=0, keepdims=True))
        pt = jnp.exp2(st - m_new).astype(BF16)
        acc_ref[comp] = jnp.exp2(m_old - m_new) * acc_ref[comp] + jnp.dot(vx, pt, preferred_element_type=F32)
        m_ref[comp] = m_new

    step = (pl.program_id(0) * pl.num_programs(1) + h) * pl.num_programs(2) + qi
    lo = lo_ref[step]
    n_pos = hi_ref[step] - lo + 1

    def key_block(b):
        j = lo + b - 1
        side = (j >= qi).astype(jnp.int32)
        return jnp.where(b == 0, qi, j + side), side

    def fill(b, dst_ref):
        j, side = key_block(b)
        for comp in range(2):
            dst_ref[comp] = scores(j, side, comp)

    def drain(b, src_ref):
        j, _ = key_block(b)
        for comp in range(2):
            softmax_pv(src_ref[comp], j, comp)

    def fill_drain(b, full_ref, empty_ref):
        j_next, side = key_block(b + 1)
        j, _ = key_block(b)
        for comp in range(2):
            empty_ref[comp] = scores(j_next, side, comp)
            softmax_pv(full_ref[comp], j, comp)

    def trip(t, carry):
        fill_drain(2 * t, sa_ref, sb_ref)
        fill_drain(2 * t + 1, sb_ref, sa_ref)
        return carry

    jpos = qi * tk + lax.broadcasted_iota(jnp.int32, (tk, tq), 0)
    ipos_t = qi * tq + lax.broadcasted_iota(jnp.int32, (tk, tq), 1)
    bias = slope * jnp.abs(jpos - ipos_t).astype(F32)
    for comp in range(2):
        sa_ref[comp] = scores(qi, 2, comp) - bias
    lax.fori_loop(0, (n_pos - 1) // 2, trip, 0)

    @pl.when(n_pos % 2 == 0)
    def _():
        fill(n_pos - 1, sb_ref)
        drain(n_pos - 2, sa_ref)
        drain(n_pos - 1, sb_ref)

    @pl.when(n_pos % 2 == 1)
    def _():
        drain(n_pos - 1, sa_ref)

    o0 = acc_ref[0, :DIFF_DV] / acc_ref[0, DIFF_DV:DIFF_DV + 1]
    o1 = acc_ref[1, :DIFF_DV] / acc_ref[1, DIFF_DV:DIFF_DV + 1]
    o = o0 - lam_ref[0] * o1
    ms = jnp.mean(o * o, axis=0, keepdims=True)
    o = o * lax.rsqrt(ms + LN_EPS) * g_ref[...] * (1.0 - lambda_init)
    o_ref[0] = o.T.astype(o_ref.dtype)


def _active_key_range(norms, slopes, tile):
    B, nb, _ = norms.shape
    nh = DIFF_HEADS * 2
    qn = (jnp.sqrt(norms[:, :, :nh]) * NORM_MARGIN).reshape(B, nb, DIFF_HEADS, 2)
    kn = (jnp.sqrt(norms[:, :, nh:2 * nh]) * NORM_MARGIN).reshape(B, nb, DIFF_HEADS, 2)
    blk = jnp.arange(nb, dtype=jnp.int32)
    gap = jnp.abs(blk[:, None] - blk[None, :])
    dmin = jnp.maximum((gap - 1) * tile + 1, 0).astype(F32)
    bound = (qn[:, :, None] * (kn[:, None, :] + kn[:, :, None])
             - (slopes[None, None, None, :, None] * dmin[None, :, :, None, None]))
    active = jnp.any(bound > -SKIP_LOG2, axis=-1) | (gap == 0)[None, :, :, None]
    lo = jnp.min(jnp.where(active, blk[None, None, :, None], nb), axis=2)
    hi = jnp.max(jnp.where(active, blk[None, None, :, None], -1), axis=2)
    to_bhi = lambda a: a.transpose(0, 2, 1).reshape(-1).astype(jnp.int32)
    return to_bhi(lo), to_bhi(hi)


def _diff_attn(proj, vt, norms, slopes, lam, g_col, lambda_init):
    B, S, _ = proj.shape
    tq = tk = min(SEQ_TILE, S)
    nk = S // tk
    pos = lax.broadcasted_iota(jnp.int32, (nk, tk, 128), 0) * tk + lax.broadcasted_iota(jnp.int32, (nk, tk, 128), 1)
    lane = lax.broadcasted_iota(jnp.int32, (nk, tk, 128), 2)
    kpos = jnp.where(lane < 3, 1, jnp.where((lane == 3) | (lane == 5), pos // POS_SPLIT,
                                            jnp.where((lane == 4) | (lane == 6), pos % POS_SPLIT, 0))).astype(BF16)
    lo, hi = _active_key_range(norms.reshape(B, nk, 128), slopes, tk)
    qb, kb = OFF_QD // 128, OFF_KD // 128
    grid_spec = pltpu.PrefetchScalarGridSpec(
        num_scalar_prefetch=2,
        grid=(B, DIFF_HEADS, S // tq),
        in_specs=[pl.BlockSpec(memory_space=pltpu.SMEM),
                  pl.BlockSpec(memory_space=pltpu.SMEM),
                  pl.BlockSpec((1, tq, 128), lambda b, h, i, lo, hi: (b, i, qb + h)),
                  pl.BlockSpec((1, nk, tk, 128), lambda b, h, i, lo, hi: (b, 0, 0, kb + h)),
                  pl.BlockSpec((nk, tk, 128), lambda b, h, i, lo, hi: (0, 0, 0)),
                  pl.BlockSpec((1, nk, DIFF_DV, tk), lambda b, h, i, lo, hi: (b, 0, h, 0)),
                  pl.BlockSpec((DIFF_DV, 1), lambda b, h, i, lo, hi: (0, 0))],
        out_specs=pl.BlockSpec((1, tq, 128), lambda b, h, i, lo, hi: (b, i, h)),
        scratch_shapes=[pltpu.VMEM((2, 3, tq, 256), BF16), pltpu.VMEM((2, 1, tq), F32),
                        pltpu.VMEM((2, DIFF_DV + ONES_ROWS, tq), F32),
                        pltpu.VMEM((2, tk, tq), F32), pltpu.VMEM((2, tk, tq), F32)],
    )
    return pl.pallas_call(
        functools.partial(_diff_attn_kernel, tq=tq, tk=tk, lambda_init=lambda_init),
        name="diff_attn",
        grid_spec=grid_spec,
        out_shape=jax.ShapeDtypeStruct((B, S, DIFF_V), BF16),
        compiler_params=_cparams(("parallel", "parallel", "parallel")),
    )(lo, hi, slopes, lam, proj, proj.reshape(B, nk, tk, -1), kpos, vt.reshape(B, nk, DIFF_V, tk), g_col)


def _dilated_kernel(slope_ref, q_ref, *refs, tl, half, n_kb, length, group, dilation):
    k_refs, v_refs, (o_ref, lse_ref) = refs[:n_kb], refs[n_kb:2 * n_kb], refs[2 * n_kb:]
    i = pl.program_id(2)
    q = q_ref[0]
    kcat = jnp.concatenate([r[0] for r in k_refs], axis=0)
    vcat = jnp.concatenate([r[0] for r in v_refs], axis=0)
    nkeys = n_kb * half
    nh = DIL_HEADS_PER_GROUP
    lane = lax.broadcasted_iota(jnp.int32, q.shape, 1)
    qstack = jnp.concatenate(
        [jnp.where((lane >= h * DIL_HD) & (lane < (h + 1) * DIL_HD), q, jnp.zeros_like(q)) for h in range(nh)], axis=0)
    row = lax.broadcasted_iota(jnp.int32, (nh * tl, 1), 0)
    slope = jnp.zeros((nh * tl, 1), F32)
    for h in range(nh):
        slope = jnp.where(row // tl == h, slope_ref[group * nh + h] * dilation, slope)
    qpos = i * tl + lax.broadcasted_iota(jnp.int32, (nh * tl, nkeys), 0) % tl
    kpos = i * tl - half + lax.broadcasted_iota(jnp.int32, (nh * tl, nkeys), 1)
    rel = jnp.abs(qpos - kpos)
    ok = (rel <= half) & (kpos >= 0) & (kpos < length)
    s = lax.dot_general(qstack, kcat, (((1,), (1,)), ((), ())), preferred_element_type=F32)
    s = jnp.where(ok, s - slope * rel.astype(F32), NEG_INF)
    m = jnp.max(s, axis=-1, keepdims=True)
    p = jnp.exp(s - m)
    l = jnp.sum(p, axis=-1, keepdims=True)
    pv = jnp.dot(p.astype(BF16), vcat, preferred_element_type=F32) / l
    lse = m + jnp.log(l)
    olane = lax.broadcasted_iota(jnp.int32, (tl, DIL_OUT), 1)
    o_acc = jnp.zeros((tl, DIL_OUT), F32)
    lse_acc = jnp.zeros((tl, DIL_OUT), F32)
    for h in range(nh):
        osel = (olane >= h * DIL_HD) & (olane < (h + 1) * DIL_HD)
        o_acc = jnp.where(osel, pv[h * tl:(h + 1) * tl], o_acc)
        lse_acc = jnp.where(osel, lse[h * tl:(h + 1) * tl], lse_acc)
    o_ref[0] = o_acc.astype(o_ref.dtype)
    lse_ref[0] = lse_acc


def _dilated_group(qkv, slopes, group, tl=128):
    B, S, _ = qkv.shape
    window, dil = DIL_CONFIGS[group]
    L = S // dil
    half = window // (2 * dil)
    tl = min(tl, L)
    assert tl % half == 0 and L % tl == 0
    per = tl // half
    n_kb = per + 2
    last_kb = L // half - 1
    pv = qkv.reshape(B, L, dil * D_DIL)
    nc = D_DIL // DIL_OUT
    qc, kc, vc = 0, 1, 2

    def kv_spec(col, n):
        def imap(b, r, i):
            return (b, jnp.clip(i * per - 1 + n, 0, last_kb), r * nc + col)
        return pl.BlockSpec((1, half, DIL_OUT), imap)

    q_spec = pl.BlockSpec((1, tl, DIL_OUT), lambda b, r, i: (b, i, r * nc + qc))
    out_spec = pl.BlockSpec((1, tl, DIL_OUT), lambda b, r, i: (b, i, r))
    o, lse = pl.pallas_call(
        functools.partial(_dilated_kernel, tl=tl, half=half, n_kb=n_kb, length=L, group=group, dilation=float(dil)),
        name=f"dilated_attn_{group}",
        grid=(B, dil, L // tl),
        in_specs=[pl.BlockSpec(memory_space=pltpu.SMEM), q_spec]
        + [kv_spec(kc, n) for n in range(n_kb)] + [kv_spec(vc, n) for n in range(n_kb)],
        out_specs=[out_spec, out_spec],
        out_shape=[jax.ShapeDtypeStruct((B, L, dil * DIL_OUT), BF16),
                   jax.ShapeDtypeStruct((B, L, dil * DIL_OUT), F32)],
        compiler_params=_cparams(("parallel", "parallel", "parallel")),
    )(slopes, *([pv] * (1 + 2 * n_kb)))
    return o.reshape(B, S, DIL_OUT), lse.reshape(B, S, DIL_OUT)


def _mem_kv_kernel(m_ref, g_ref, b_ref, w_ref, o_ref):
    mn = _ln(m_ref[...], g_ref[...], b_ref[...]).astype(BF16)
    o_ref[...] = jnp.dot(mn, w_ref[...], preferred_element_type=F32).astype(o_ref.dtype)


def _mem_kv(mem2d, g, b, w, tm=256):
    T, D = mem2d.shape
    N = w.shape[1]
    return pl.pallas_call(
        _mem_kv_kernel,
        name="mem_kv",
        grid=(T // tm,),
        in_specs=[pl.BlockSpec((tm, D), lambda i: (i, 0)),
                  pl.BlockSpec((1, D), lambda i: (0, 0)),
                  pl.BlockSpec((1, D), lambda i: (0, 0)),
                  pl.BlockSpec((D, N), lambda i: (0, 0))],
        out_specs=pl.BlockSpec((tm, N), lambda i: (i, 0)),
        out_shape=jax.ShapeDtypeStruct((T, N), BF16),
        compiler_params=_cparams(("parallel",)),
    )(mem2d, g, b, w)


def _mem_attn_kernel(q_ref, kv_ref, o_ref):
    q = q_ref[0]
    k = kv_ref[0, :, :MEM_Q]
    v = kv_ref[0, :, MEM_Q:]
    lane = lax.broadcasted_iota(jnp.int32, q.shape, 1)
    o_acc = jnp.zeros(q.shape, F32)
    for h in range(MEM_HEADS):
        sel = (lane >= h * MEM_HD) & (lane < (h + 1) * MEM_HD)
        qz = jnp.where(sel, q, jnp.zeros_like(q))
        s = lax.dot_general(qz, k, (((1,), (1,)), ((), ())), preferred_element_type=F32)
        m = jnp.max(s, axis=-1, keepdims=True)
        p = jnp.exp(s - m)
        l = jnp.sum(p, axis=-1, keepdims=True)
        pv = jnp.dot(p.astype(BF16), v, preferred_element_type=F32) / l
        o_acc = jnp.where(sel, pv, o_acc)
    o_ref[0] = o_acc.astype(o_ref.dtype)


def _mem_attn(proj, kv, tm=512):
    B, S, _ = proj.shape
    M = kv.shape[1]
    tm = min(tm, S)
    qc = OFF_QM // MEM_Q
    return pl.pallas_call(
        _mem_attn_kernel,
        name="mem_attn",
        grid=(B, S // tm),
        in_specs=[pl.BlockSpec((1, tm, MEM_Q), lambda b, i: (b, i, qc)),
                  pl.BlockSpec((1, M, 2 * MEM_Q), lambda b, i: (b, 0, 0))],
        out_specs=pl.BlockSpec((1, tm, MEM_Q), lambda b, i: (b, i, 0)),
        out_shape=jax.ShapeDtypeStruct((B, S, MEM_Q), BF16),
        compiler_params=_cparams(("parallel", "parallel")),
    )(proj, kv)


HI16 = -65536


def _pack_halves(x):
    half = x.shape[1] // 2
    lo = lax.bitcast_convert_type(x[:, :half].astype(BF16).astype(F32), jnp.int32)
    hi = lax.bitcast_convert_type(x[:, half:].astype(BF16).astype(F32), jnp.int32)
    return lax.shift_right_logical(lo, 16) | (hi & HI16)


def _unpack_halves(r):
    lo = lax.bitcast_convert_type(lax.shift_left(r, 16), F32).astype(BF16)
    hi = lax.bitcast_convert_type(r & HI16, F32).astype(BF16)
    return lo, hi


def _merge_kernel(x_ref, od_ref, o0_ref, o1_ref, o2_ref, l0_ref, l1_ref, l2_ref, om_ref,
                  lng_ref, lnb_ref, wg_ref, bg_ref, wbd_ref, wbl_ref, wbm_ref, wo_ref, g1_ref, b1_ref,
                  x1_ref):
    D = x_ref.shape[1]
    xn = _ln(x_ref[...], lng_ref[...], lnb_ref[...])
    xb = xn.astype(BF16)
    l0, l1, l2 = l0_ref[...], l1_ref[...], l2_ref[...]
    mx = jnp.maximum(jnp.maximum(l0, l1), l2)
    e0, e1, e2 = jnp.exp(l0 - mx), jnp.exp(l1 - mx), jnp.exp(l2 - mx)
    ol = (e0 * o0_ref[...].astype(F32) + e1 * o1_ref[...].astype(F32) + e2 * o2_ref[...].astype(F32)) / (e0 + e1 + e2)
    branches = (jnp.dot(od_ref[...], wbd_ref[...], preferred_element_type=F32),
                jnp.dot(ol.astype(BF16), wbl_ref[...], preferred_element_type=F32),
                jnp.dot(om_ref[...], wbm_ref[...], preferred_element_type=F32))
    merged = jnp.zeros(xn.shape, F32)
    for n, br in enumerate(branches):
        cols = slice(n * D, (n + 1) * D)
        gate = jax.nn.sigmoid(jnp.dot(xb, wg_ref[:, cols], preferred_element_type=F32) + bg_ref[:, cols])
        merged = merged + gate * br
    y = DEEPNORM_ALPHA * xn + jnp.dot(merged.astype(BF16), wo_ref[...], preferred_element_type=F32)
    x1_ref[...] = _ln(y, g1_ref[...], b1_ref[...])


def _merge(x, od, odil, lses, om, lng, lnb, wg, bg, wbd, wbl, wbm, wo, g1, b1, tm=256):
    T, D = x.shape
    row = lambda w: pl.BlockSpec((tm, w), lambda i: (i, 0))
    full = lambda a: pl.BlockSpec(a.shape, lambda i: (0,) * a.ndim)
    params = (lng, lnb, wg, bg, wbd, wbl, wbm, wo, g1, b1)
    return pl.pallas_call(
        _merge_kernel,
        name="merge_out",
        grid=(T // tm,),
        in_specs=[row(D), row(D)] + [row(DIL_OUT)] * 7 + [full(a) for a in params],
        out_specs=row(D),
        out_shape=jax.ShapeDtypeStruct((T, D), F32),
        compiler_params=_cparams(("parallel",)),
    )(x, od, *odil, *lses, om, *params)


def _router_kernel(x_ref, w_ref, bias_ref, idx_ref, xr_ref, cnt_ref):
    x = x_ref[...]
    logits = jnp.dot(x.astype(BF16), w_ref[...], preferred_element_type=F32)
    scores = jax.nn.sigmoid(logits)
    sel = scores + bias_ref[...]
    tm, E = sel.shape
    lane = lax.broadcasted_iota(jnp.int32, (tm, E), 1)
    olane = lax.broadcasted_iota(jnp.int32, idx_ref.shape, 1)
    idx_out = jnp.zeros(idx_ref.shape, jnp.int32)
    chosen = jnp.zeros((tm, E), jnp.bool_)
    for k in range(TOP_K):
        mx = jnp.max(sel, axis=-1, keepdims=True)
        idx = jnp.min(jnp.where(sel == mx, lane, E), axis=-1, keepdims=True)
        hit = lane == idx
        chosen = chosen | hit
        sel = jnp.where(hit, -jnp.inf, sel)
        idx_out = jnp.where(olane == k, idx, idx_out)
    picked = jnp.where(chosen, scores, 0.0)
    idx_ref[...] = idx_out
    cnt_ref[0] = jnp.sum(chosen.astype(jnp.int32), axis=0, keepdims=True)
    rw = picked / jnp.sum(picked, axis=-1, keepdims=True) * ROUTED_SCALE
    words = jnp.concatenate([_pack_halves(x), lax.bitcast_convert_type(rw, jnp.int32)], axis=1)
    pad = ROW_TILE_WORDS - words.shape[1]
    if pad:
        words = jnp.concatenate([words, jnp.zeros((tm, pad), jnp.int32)], axis=1)
    for s in range(8):
        xr_ref[pl.ds(s, tm, stride=8), :] = words[:, s * 128:(s + 1) * 128]


def _router(x1, w, bias, tm=512):
    T, D = x1.shape
    E = w.shape[1]
    return pl.pallas_call(
        _router_kernel,
        name="router",
        grid=(T // tm,),
        in_specs=[pl.BlockSpec((tm, D), lambda i: (i, 0)),
                  pl.BlockSpec((D, E), lambda i: (0, 0)),
                  pl.BlockSpec((1, E), lambda i: (0, 0))],
        out_specs=[pl.BlockSpec((tm, 128), lambda i: (i, 0)), pl.BlockSpec((tm * 8, 128), lambda i: (i, 0)),
                   pl.BlockSpec((1, 1, E), lambda i: (i, 0, 0))],
        out_shape=[jax.ShapeDtypeStruct((T, 128), jnp.int32), jax.ShapeDtypeStruct((T * 8, 128), jnp.int32),
                   jax.ShapeDtypeStruct((T // tm, 1, E), jnp.int32)],
        compiler_params=_cparams(("parallel",)),
    )(x1, w, bias)


def _moe_kernel(offs_ref, tok_ref, xr_ref, wg_ref, wu_ref, wd_ref, out_ref, rows_ref, y_ref, *, chunk, n_experts):
    tile = pl.program_id(0)
    step = pl.program_id(1)

    @pl.when(step == 0)
    def _():
        out_ref[...] = jnp.zeros(out_ref.shape, F32)

    @pl.when((tile == 0) & (step == 0))
    def _():
        rows_ref[...] = jnp.zeros(rows_ref.shape, rows_ref.dtype)

    half = wg_ref.shape[1] // 2
    n_together = rows_ref.shape[0]
    lax.fori_loop(0, wg_ref.shape[0] // n_together, functools.partial(
        _moe_expert_group, offs_ref=offs_ref, tok_ref=tok_ref, xr_ref=xr_ref, wg_ref=wg_ref, wu_ref=wu_ref,
        wd_ref=wd_ref, out_ref=out_ref, rows_ref=rows_ref, y_ref=y_ref, chunk=chunk, n_experts=n_experts,
        half=half, tile=tile, first=step * wg_ref.shape[0]), 0)


def _moe_expert_group(gg, carry, *, offs_ref, tok_ref, xr_ref, wg_ref, wu_ref, wd_ref, out_ref, rows_ref, y_ref,
                      chunk, n_experts, half, tile, first):
    n_together = rows_ref.shape[0]
    ees = [gg * n_together + k for k in range(n_together)]
    experts = [first + ee for ee in ees]
    starts = [offs_ref[tile * (n_experts + 1) + e] for e in experts]
    counts = [offs_ref[tile * (n_experts + 1) + e + 1] - st for e, st in zip(experts, starts)]
    n_act = half // 128
    n_rw = -(-n_experts // 128)

    def one_chunk(c, carry):
        cnts = [jnp.clip(n - c * chunk, 0, chunk) for n in counts]
        bases = [jnp.where(cnt > 0, st + c * chunk, 0) for st, cnt in zip(starts, cnts)]

        for k in range(n_together):
            for i in range(chunk):
                rows_ref[k, i * 8:(i + 1) * 8, :] = xr_ref[tok_ref[0, 0, bases[k] + i]]

        for k in range(n_together):
            ee = ees[k]

            def slab(s):
                return rows_ref[k, pl.ds(s, chunk, stride=8), :]

            lo, hi = _unpack_halves(jnp.concatenate([slab(s) for s in range(n_act)], axis=1))
            g = (jnp.dot(lo, wg_ref[ee, :half], preferred_element_type=F32)
                 + jnp.dot(hi, wg_ref[ee, half:], preferred_element_type=F32))
            u = (jnp.dot(lo, wu_ref[ee, :half], preferred_element_type=F32)
                 + jnp.dot(hi, wu_ref[ee, half:], preferred_element_type=F32))
            wrow = lax.bitcast_convert_type(
                jnp.concatenate([slab(n_act + s) for s in range(n_rw)], axis=1)[:, :n_experts], F32)
            lane = lax.broadcasted_iota(jnp.int32, wrow.shape, 1)
            wcol = jnp.sum(jnp.where(lane == experts[k], wrow, 0.0), axis=1, keepdims=True)
            wcol = jnp.where(lax.broadcasted_iota(jnp.int32, wcol.shape, 0) < cnts[k], wcol, 0.0)
            hid = (g * jax.nn.sigmoid(g)) * u * wcol
            y = jnp.dot(hid.astype(BF16), wd_ref[ee], preferred_element_type=F32)
            for s in range(y.shape[1] // 128):
                y_ref[k, pl.ds(s, chunk, stride=8), :] = y[:, s * 128:(s + 1) * 128]

        n_tok = out_ref.shape[0] - SPARE_ROWS
        for k in range(n_together):
            for g in range(chunk // ROW_GROUP):
                rows = []
                for s in range(ROW_GROUP):
                    i = g * ROW_GROUP + s
                    rows.append(jnp.where(i < cnts[k], tok_ref[0, 0, bases[k] + i], n_tok + s))
                olds = [out_ref[t] for t in rows]
                for s in range(ROW_GROUP):
                    i = g * ROW_GROUP + s
                    out_ref[rows[s]] = olds[s] + y_ref[k, i * 8:(i + 1) * 8, :]
        return carry

    n_max = functools.reduce(jnp.maximum, counts)
    lax.fori_loop(0, (n_max + chunk - 1) // chunk, one_chunk, 0)
    return carry


def _moe_experts(offs, tok, xr, wg, wu, wd, tile):
    T = xr.shape[0]
    E, D, F = wg.shape
    assert D == ROW_TILE_WORDS and D // 2 + E <= ROW_TILE_WORDS and (D // 2) % 128 == 0
    nt = T // tile
    eps = math.gcd(E, EXPERTS_PER_STEP)
    grid_spec = pltpu.PrefetchScalarGridSpec(
        num_scalar_prefetch=1,
        grid=(nt, E // eps),
        in_specs=[pl.BlockSpec((1, 1, tok.shape[2]), lambda t, e, offs: (t, 0, 0), memory_space=pltpu.SMEM),
                  pl.BlockSpec((tile, 8, 128), lambda t, e, offs: (t, 0, 0), pipeline_mode=pl.Buffered(1)),
                  pl.BlockSpec((eps, D, F), lambda t, e, offs: (e, 0, 0)),
                  pl.BlockSpec((eps, D, F), lambda t, e, offs: (e, 0, 0)),
                  pl.BlockSpec((eps, F, D), lambda t, e, offs: (e, 0, 0))],
        out_specs=pl.BlockSpec((tile + SPARE_ROWS, 8, 128), lambda t, e, offs: (t, 0, 0),
                               pipeline_mode=pl.Buffered(1)),
        scratch_shapes=[pltpu.VMEM((EXPERTS_TOGETHER, EXPERT_ROWS * 8, 128), jnp.int32),
                        pltpu.VMEM((EXPERTS_TOGETHER, EXPERT_ROWS * 8, 128), F32)],
    )
    return pl.pallas_call(
        functools.partial(_moe_kernel, chunk=EXPERT_ROWS, n_experts=E),
        name="moe_experts",
        grid_spec=grid_spec,
        out_shape=jax.ShapeDtypeStruct((nt * (tile + SPARE_ROWS), 8, 128), F32),
        compiler_params=_cparams(("arbitrary", "arbitrary")),
    )(offs, tok, xr, wg, wu, wd)


def _final_kernel(x1_ref, r_ref, wg_ref, wu_ref, wd_ref, g_ref, b_ref, o_ref):
    xb = x1_ref[...].astype(BF16)
    g = jnp.dot(xb, wg_ref[...], preferred_element_type=F32)
    u = jnp.dot(xb, wu_ref[...], preferred_element_type=F32)
    hid = (g * jax.nn.sigmoid(g)) * u
    shared = jnp.dot(hid.astype(BF16), wd_ref[...], preferred_element_type=F32)
    tm = x1_ref.shape[0]
    routed = jnp.concatenate([r_ref[pl.ds(s, tm, stride=8), :] for s in range(8)], axis=1)
    y = DEEPNORM_ALPHA * x1_ref[...] + (routed + shared)
    o_ref[...] = _ln(y, g_ref[...], b_ref[...])


def _final(x1, routed, moe_tile, wg, wu, wd, g, b, tm=SPARE_ROWS):
    T, D = x1.shape
    per = moe_tile // tm
    row = pl.BlockSpec((tm, D), lambda i: (i, 0))
    full = lambda a: pl.BlockSpec(a.shape, lambda i: (0,) * a.ndim)
    params = (wg, wu, wd, g, b)
    return pl.pallas_call(
        _final_kernel,
        name="shared_final",
        grid=(T // tm,),
        in_specs=[row, pl.BlockSpec((tm * 8, 128), lambda i: (i // per * (per + 1) + i % per, 0))]
        + [full(a) for a in params],
        out_specs=row,
        out_shape=jax.ShapeDtypeStruct((T, D), F32),
        compiler_params=_cparams(("parallel",)),
    )(x1, routed.reshape(-1, 128), *params)


def _moe_routed(xr, top_idx, counts, wg, wu, wd):
    T = xr.shape[0]
    E = wg.shape[0]
    tile = min(MOE_TILE, T)
    nt = T // tile
    n_assign = tile * TOP_K
    pos = lax.broadcasted_iota(jnp.int32, (nt, n_assign), 1)
    order = jnp.sort(top_idx.reshape(nt, n_assign) * n_assign + pos, axis=1) % n_assign
    tok = jnp.pad(order // TOP_K, ((0, 0), (0, EXPERT_ROWS))).reshape(nt, 1, -1)
    per_tile = counts.reshape(nt, -1, E).sum(axis=1)
    offs = jnp.concatenate([jnp.zeros((nt, 1), jnp.int32), jnp.cumsum(per_tile, axis=1, dtype=jnp.int32)], axis=1)
    return _moe_experts(offs.reshape(-1), tok, xr, wg, wu, wd, tile), tile


def _trunk(x, mem, p):
    B, S, D = x.shape
    T = B * S
    x2 = x.reshape(T, D)
    proj, d0, d1, d2, vt, norms = _ln_proj(x2, p['ln_in_g'], p['ln_in_b'], p['w_in'], p['w_vt'])
    proj = proj.reshape(B, S, D_MAIN)
    od = _diff_attn(proj, vt, norms, p['diff_slopes'], p['lam'], p['diff_subln_g'], p['lambda_init'])
    dil = [_dilated_group(d.reshape(B, S, D_DIL), p['dil_slopes'], g) for g, d in enumerate((d0, d1, d2))]
    kv = _mem_kv(mem.reshape(-1, D), p['ln_mem_g'], p['ln_mem_b'], p['w_mem_kv']).reshape(B, mem.shape[1], 2 * MEM_Q)
    om = _mem_attn(proj, kv)
    x1 = _merge(x2, od.reshape(T, DIFF_V), [o.reshape(T, DIL_OUT) for o, _ in dil],
                [l.reshape(T, DIL_OUT) for _, l in dil], om.reshape(T, MEM_Q),
                p['ln_in_g'], p['ln_in_b'], p['w_gate'], p['b_gate'], p['w_br_diff'], p['w_br_dil'],
                p['w_br_mem'], p['w_out'], p['ln1_g'], p['ln1_b'])
    idx, xr, counts = _router(x1, p['w_router'], p['router_bias'])
    routed, moe_tile = _moe_routed(xr.reshape(T, 8, 128), idx[:, :TOP_K], counts,
                                   p['w_e_gate'], p['w_e_up'], p['w_e_down'])
    y = _final(x1, routed, moe_tile, p['w_s_gate'], p['w_s_up'], p['w_s_down'], p['ln2_g'], p['ln2_b'])
    return y.reshape(B, S, D)


def kernel(x_prompt, x_sample, mem_prompt, mem_sample, ln_in_g, ln_in_b, ln_mem_g, ln_mem_b, w_in, w_gate, b_gate,
           lambda_q1, lambda_k1, lambda_q2, lambda_k2, diff_subln_g, w_br_diff, w_br_dil, w_br_mem, w_mem_kv,
           w_out, ln1_g, ln1_b, w_router, router_bias, w_e_gate, w_e_up, w_e_down, w_s_gate, w_s_up, w_s_down,
           ln2_g, ln2_b):
    l = 0
    lambda_init = 0.8 - 0.6 * math.exp(-0.3 * l)
    lam = (jnp.exp(jnp.sum(lambda_q1[l] * lambda_k1[l])) - jnp.exp(jnp.sum(lambda_q2[l] * lambda_k2[l]))
           + lambda_init).astype(F32).reshape(1)
    colscale = np.ones((D_IN,), np.float32)
    for off, width, hd in ((0, DIFF_QK, DIFF_DK), (SRC_QL, DIL_W, DIL_HD), (SRC_QM, MEM_Q, MEM_HD)):
        colscale[off:off + width] = hd ** -0.5
    colscale[:DIFF_QK] *= LOG2E
    w_all = (w_in[l] * colscale).astype(BF16)
    cols = lambda off, width: w_all[:, off:off + width]
    w_groups = [jnp.concatenate([cols(src + g * DIL_OUT, DIL_OUT) for src in (SRC_QL, SRC_KL, SRC_VL)], axis=1)
                for g in range(len(DIL_CONFIGS))]
    row = lambda a: a.reshape(1, -1).astype(F32)
    p = dict(
        ln_in_g=row(ln_in_g), ln_in_b=row(ln_in_b), ln_mem_g=row(ln_mem_g[l]), ln_mem_b=row(ln_mem_b[l]),
        w_in=jnp.concatenate([cols(0, SRC_VD), cols(SRC_QM, MEM_Q)] + w_groups, axis=1),
        w_vt=cols(SRC_VD, DIFF_V).T,
        w_gate=w_gate[l].astype(BF16), b_gate=row(b_gate[l]),
        lam=lam, lambda_init=lambda_init, diff_subln_g=diff_subln_g[l].reshape(-1, 1).astype(F32),
        diff_slopes=jnp.asarray(_alibi_slopes(DIFF_HEADS) * np.float32(LOG2E)), dil_slopes=jnp.asarray(_alibi_slopes(N_DIL_HEADS)),
        w_br_diff=w_br_diff[l].astype(BF16), w_br_dil=w_br_dil[l].astype(BF16), w_br_mem=w_br_mem[l].astype(BF16),
        w_mem_kv=w_mem_kv[l].astype(BF16), w_out=w_out[l].astype(BF16), ln1_g=row(ln1_g[l]), ln1_b=row(ln1_b[l]),
        w_router=w_router[l].astype(BF16), router_bias=row(router_bias[l]),
        w_e_gate=w_e_gate[l].astype(BF16), w_e_up=w_e_up[l].astype(BF16), w_e_down=w_e_down[l].astype(BF16),
        w_s_gate=w_s_gate[l].astype(BF16), w_s_up=w_s_up[l].astype(BF16), w_s_down=w_s_down[l].astype(BF16),
        ln2_g=row(ln2_g[l]), ln2_b=row(ln2_b[l]),
    )
    return (_trunk(x_prompt, mem_prompt, p), _trunk(x_sample, mem_sample, p))
```
